```python
import jax, jax.numpy as jnp
from jax import lax
import numpy as np

D_MODEL = 1024
BATCH = 8
SEQ = 2048
DEPTH = 1
DEC_BATCH = 128
DEC_SEQ = 4
PAST_LEN = 8192
PAGE_SIZE = 128

N_META = 16
Q_BLOCK = 128
SB_HEADS = 8
SB_HEAD_DIM = 64
SB_WIDTH = SB_HEADS * SB_HEAD_DIM
MLA_HEADS = 8
MLA_NOPE = 64
MLA_ROPE = 32
MLA_V = 64
MLA_WIDTH = MLA_HEADS * MLA_V
Q_LORA = 384
KV_LORA = 256
MLA_SCALE = (MLA_NOPE + MLA_ROPE) ** -0.5
ROPE_THETA = 10000.0
RMS_EPS = 1e-6
NEG_INF = -1e30
COL_SPLITS = (SB_WIDTH, SB_WIDTH, SB_WIDTH, SB_WIDTH, Q_LORA, KV_LORA, MLA_ROPE, MLA_WIDTH, D_MODEL, D_MODEL)
N_IN = sum(COL_SPLITS)

kernel_name = 'hybrid_stickbreak_mla_decoder_step'


def rmsnorm(x, g):
    xf = x.astype(jnp.float32)
    y = xf * lax.rsqrt(jnp.mean(xf * xf, axis=-1, keepdims=True) + RMS_EPS)
    return (y * g.astype(jnp.float32)).astype(x.dtype)


def rope(x, pos):
    half = MLA_ROPE // 2
    inv = ROPE_THETA ** (-jnp.arange(half, dtype=jnp.float32) * 2.0 / MLA_ROPE)
    ang = pos.astype(jnp.float32)[:, None] * inv[None, :]
    ang = ang.reshape(ang.shape[:1] + (1,) * (x.ndim - 3) + (half,))
    cos, sin = jnp.cos(ang), jnp.sin(ang)
    xf = x.astype(jnp.float32).reshape(x.shape[:-1] + (half, 2))
    x1, x2 = xf[..., 0], xf[..., 1]
    out = jnp.stack([x1 * cos - x2 * sin, x1 * sin + x2 * cos], axis=-1).reshape(x.shape)
    return out.astype(x.dtype)


def split_cols(proj):
    idx, acc = [], 0
    for w in COL_SPLITS[:-1]:
        acc += w
        idx.append(acc)
    return jnp.split(proj, idx, axis=-1)


def mixer_inputs(x, pos, g_pre, w_in, g_qnorm, w_uq, g_kvnorm, w_uk):
    b, l = x.shape[0], x.shape[1]
    h = rmsnorm(x, g_pre)
    proj = jnp.einsum('bld,dn->bln', h, w_in)
    q_sb, k_sb, v_sb, z_a, c_q, c_kv, k_rope, z_b, gate_a, gate_b = split_cols(proj)
    q_sb = q_sb.reshape(b, l, SB_HEADS, SB_HEAD_DIM)
    k_sb = k_sb.reshape(b, l, SB_HEADS, SB_HEAD_DIM)
    v_sb = v_sb.reshape(b, l, SB_HEADS, SB_HEAD_DIM)
    q_mla = jnp.einsum('blq,qn->bln', rmsnorm(c_q, g_qnorm), w_uq).reshape(b, l, MLA_HEADS, MLA_NOPE + MLA_ROPE)
    q_nope = q_mla[..., :MLA_NOPE]
    q_rope = rope(q_mla[..., MLA_NOPE:], pos)
    q_lat = jnp.einsum('blhn,hcn->blhc', q_nope, w_uk)
    c_kv = rmsnorm(c_kv, g_kvnorm)
    k_rope = rope(k_rope, pos)
    return q_sb, k_sb, v_sb, q_lat, q_rope, c_kv, k_rope, z_a, z_b, gate_a, gate_b


def sb_attend(q, q_pos, k, v, k_pos):
    z = jnp.einsum('bqhd,bkhd->bhqk', q, k, preferred_element_type=jnp.float32) * (SB_HEAD_DIM ** -0.5)
    mask = k_pos[None, :] < q_pos[:, None]
    log_keep = jnp.where(mask, -jax.nn.softplus(z), 0.0)
    later = lax.cumsum(log_keep, axis=3, reverse=True) - log_keep
    w = jnp.where(mask, jnp.exp(jax.nn.log_sigmoid(z) + later), 0.0)
    return jnp.einsum('bhqk,bkhd->bqhd', w.astype(v.dtype), v, preferred_element_type=jnp.float32)


def mla_attend(q_lat, q_rope, q_pos, c_kv, k_rope, k_pos, w_uv):
    s = (jnp.einsum('bqhc,bkc->bhqk', q_lat, c_kv, preferred_element_type=jnp.float32)
         + jnp.einsum('bqhr,bkr->bhqk', q_rope, k_rope, preferred_element_type=jnp.float32)) * MLA_SCALE
    s = jnp.where(k_pos[None, :] <= q_pos[:, None], s, NEG_INF)
    p = jax.nn.softmax(s, axis=-1)
    o_lat = jnp.einsum('bhqk,bkc->bqhc', p.astype(c_kv.dtype), c_kv, preferred_element_type=jnp.float32)
    return jnp.einsum('bqhc,hcv->bqhv', o_lat, w_uv.astype(jnp.float32))


def sweep_query_blocks(attend, qs, q_pos):
    L = q_pos.shape[0]
    pad = (-L) % Q_BLOCK
    nb = (L + pad) // Q_BLOCK

    def to_blocks(a):
        a = jnp.pad(a, [(0, 0), (pad, 0)] + [(0, 0)] * (a.ndim - 2))
        a = a.reshape((a.shape[0], nb, Q_BLOCK) + a.shape[2:])
        return jnp.moveaxis(a, 1, 0)

    pos_b = jnp.arange(-pad, L, dtype=jnp.int32).reshape(nb, Q_BLOCK)
    out = lax.map(lambda args: attend(*args[0], args[1]), (tuple(to_blocks(a) for a in qs), pos_b))
    out = jnp.moveaxis(out, 0, 1)
    out = out.reshape((out.shape[0], nb * Q_BLOCK) + out.shape[3:])
    return out[:, pad:]


def mixer_outputs(x, o_a, o_b, z_a, z_b, gate_a, gate_b, w_proj_a, w_proj_b, w_out, g_post):
    b, l = x.shape[0], x.shape[1]
    o_a = o_a.reshape(b, l, SB_WIDTH).astype(x.dtype) * jax.nn.silu(z_a)
    o_b = o_b.reshape(b, l, MLA_WIDTH).astype(x.dtype) * jax.nn.silu(z_b)
    merged = (jax.nn.sigmoid(gate_a) * jnp.einsum('blw,wd->bld', o_a, w_proj_a)
              + jax.nn.sigmoid(gate_b) * jnp.einsum('blw,wd->bld', o_b, w_proj_b))
    return x + rmsnorm(jnp.einsum('bld,de->ble', merged, w_out), g_post)


def gather_pages(pool, layer, page_table):
    g = pool[layer, page_table]
    return g.reshape((page_table.shape[0], page_table.shape[1] * pool.shape[2]) + pool.shape[3:])


def setup_inputs(seed: int = 0) -> dict:
    key = jax.random.key(seed)
    ks = jax.random.split(key, 24)
    n_pages = PAST_LEN // PAGE_SIZE
    n_used = DEC_BATCH * n_pages
    n_pool = (5 * n_used) // 4
    nrm = lambda k, shape, scale=1.0: jax.random.normal(k, shape, jnp.float32) * scale
    gain = lambda k, n: 1.0 + 0.02 * jax.random.normal(k, (DEPTH, n), jnp.float32)
    page_table = jax.random.permutation(ks[6], n_pool)[:n_used].reshape(DEC_BATCH, n_pages).astype(jnp.int32)
    return {
        'x_prompt': nrm(ks[0], (BATCH, SEQ, D_MODEL)),
        'x_sample': nrm(ks[1], (DEC_BATCH, DEC_SEQ, D_MODEL)),
        'cache_sb_k': nrm(ks[2], (DEPTH, n_pool, PAGE_SIZE, SB_HEADS, SB_HEAD_DIM)),
        'cache_sb_v': nrm(ks[3], (DEPTH, n_pool, PAGE_SIZE, SB_HEADS, SB_HEAD_DIM)),
        'cache_mla_ckv': nrm(ks[4], (DEPTH, n_pool, PAGE_SIZE, KV_LORA)),
        'cache_mla_krope': nrm(ks[5], (DEPTH, n_pool, PAGE_SIZE, MLA_ROPE)),
        'page_table': page_table,
        'meta_tokens': nrm(ks[7], (N_META, D_MODEL)),
        'g_pre': gain(ks[8], D_MODEL),
        'w_in': nrm(ks[9], (DEPTH, D_MODEL, N_IN), D_MODEL ** -0.5),
        'g_qnorm': gain(ks[10], Q_LORA),
        'w_uq': nrm(ks[11], (DEPTH, Q_LORA, MLA_HEADS * (MLA_NOPE + MLA_ROPE)), Q_LORA ** -0.5),
        'g_kvnorm': gain(ks[12], KV_LORA),
        'w_uk': nrm(ks[13], (DEPTH, MLA_HEADS, KV_LORA, MLA_NOPE), KV_LORA ** -0.5),
        'w_uv': nrm(ks[14], (DEPTH, MLA_HEADS, KV_LORA, MLA_V), KV_LORA ** -0.5),
        'w_proj_a': nrm(ks[15], (DEPTH, SB_WIDTH, D_MODEL), SB_WIDTH ** -0.5),
        'w_proj_b': nrm(ks[16], (DEPTH, MLA_WIDTH, D_MODEL), MLA_WIDTH ** -0.5),
        'w_out': nrm(ks[17], (DEPTH, D_MODEL, D_MODEL), D_MODEL ** -0.5),
        'g_post': gain(ks[18], D_MODEL),
    }


def reference(x_prompt, x_sample, cache_sb_k, cache_sb_v, cache_mla_ckv, cache_mla_krope, page_table,
              meta_tokens, g_pre, w_in, g_qnorm, w_uq, g_kvnorm, w_uk, w_uv, w_proj_a, w_proj_b, w_out, g_post):
    b = x_prompt.shape[0]
    meta = jnp.broadcast_to(meta_tokens.astype(x_prompt.dtype)[None], (b, N_META, meta_tokens.shape[-1]))
    xp = jnp.concatenate([meta, x_prompt], axis=1)
    L = xp.shape[1]
    pos_p = jnp.arange(L, dtype=jnp.int32)

    xs = x_sample
    dec_seq = xs.shape[1]
    past_len = page_table.shape[1] * cache_sb_k.shape[2]
    pos_s = past_len + jnp.arange(dec_seq, dtype=jnp.int32)
    kpos_s = jnp.arange(past_len + dec_seq, dtype=jnp.int32)

    sbk_p, sbv_p, ckv_p, kr_p = [], [], [], []
    sbk_s, sbv_s, ckv_s, kr_s = [], [], [], []
    for l in range(DEPTH):
        q_sb, k_sb, v_sb, q_lat, q_rope, c_kv, k_rope, z_a, z_b, gate_a, gate_b = mixer_inputs(
            xp, pos_p, g_pre[l], w_in[l], g_qnorm[l], w_uq[l], g_kvnorm[l], w_uk[l])
        o_a = sweep_query_blocks(lambda q, qp: sb_attend(q, qp, k_sb, v_sb, pos_p), (q_sb,), pos_p)
        w_uv_l = w_uv[l]
        o_b = sweep_query_blocks(
            lambda ql, qr, qp: mla_attend(ql, qr, qp, c_kv, k_rope, pos_p, w_uv_l), (q_lat, q_rope), pos_p)
        xp = mixer_outputs(xp, o_a, o_b, z_a, z_b, gate_a, gate_b, w_proj_a[l], w_proj_b[l], w_out[l], g_post[l])
        sbk_p.append(k_sb); sbv_p.append(v_sb); ckv_p.append(c_kv); kr_p.append(k_rope)

        q_sb, k_sb, v_sb, q_lat, q_rope, c_kv, k_rope, z_a, z_b, gate_a, gate_b = mixer_inputs(
            xs, pos_s, g_pre[l], w_in[l], g_qnorm[l], w_uq[l], g_kvnorm[l], w_uk[l])
        k_all = jnp.concatenate([gather_pages(cache_sb_k, l, page_table).astype(k_sb.dtype), k_sb], axis=1)
        v_all = jnp.concatenate([gather_pages(cache_sb_v, l, page_table).astype(v_sb.dtype), v_sb], axis=1)
        ckv_all = jnp.concatenate([gather_pages(cache_mla_ckv, l, page_table).astype(c_kv.dtype), c_kv], axis=1)
        kr_all = jnp.concatenate([gather_pages(cache_mla_krope, l, page_table).astype(k_rope.dtype), k_rope], axis=1)
        o_a = sb_attend(q_sb, pos_s, k_all, v_all, kpos_s)
        o_b = mla_attend(q_lat, q_rope, pos_s, ckv_all, kr_all, kpos_s, w_uv[l])
        xs = mixer_outputs(xs, o_a, o_b, z_a, z_b, gate_a, gate_b, w_proj_a[l], w_proj_b[l], w_out[l], g_post[l])
        sbk_s.append(k_sb); sbv_s.append(v_sb); ckv_s.append(c_kv); kr_s.append(k_rope)

    y_prompt = xp[:, N_META:]
    y_sample = xs
    return (y_prompt, y_sample,
            jnp.stack(sbk_p, 0), jnp.stack(sbv_p, 0), jnp.stack(ckv_p, 0), jnp.stack(kr_p, 0),
            jnp.stack(sbk_s, 0), jnp.stack(sbv_s, 0), jnp.stack(ckv_s, 0), jnp.stack(kr_s, 0))
```

```python
import functools

import jax
import jax.numpy as jnp
from jax import lax
from jax.experimental import pallas as pl
from jax.experimental.pallas import tpu as pltpu

F32 = jnp.float32
BF16 = jnp.bfloat16

N_META = 16
SB_HEADS = 8
SB_HEAD_DIM = 64
SB_WIDTH = SB_HEADS * SB_HEAD_DIM
MLA_HEADS = 8
MLA_NOPE = 64
MLA_ROPE = 32
MLA_V = 64
MLA_WIDTH = MLA_HEADS * MLA_V
Q_LORA = 384
KV_LORA = 256
D_MODEL = 1024
MLA_SCALE = (MLA_NOPE + MLA_ROPE) ** -0.5
SB_SCALE = SB_HEAD_DIM ** -0.5
ROPE_THETA = 10000.0
RMS_EPS = 1e-6
NEG_INF = -1e30

ROPE_TILED = MLA_HEADS * MLA_ROPE
NOPE_PAD = 128

_C_Q, _C_K, _C_V, _C_ZA = 0, 512, 1024, 1536
_C_CQ = 2048
_C_CKV = _C_CQ + Q_LORA
_C_ZB = _C_CKV + KV_LORA
_C_GA = _C_ZB + MLA_WIDTH
_C_GB = _C_GA + D_MODEL
_C_KR = _C_GB + D_MODEL
_C_KRS = _C_KR + ROPE_TILED
_N_AUG = _C_KRS + ROPE_TILED

VMEM_LIMIT = 56 * 1024 * 1024

PROJ_TM = 256
OUT_TM = 512
SB_T = 256
MLA_TQ = 128
MLA_TK = 256
CUM_B = 256
PAGES_PER_CHUNK = 8


def _dot(a, b):
    return jnp.dot(a, b, preferred_element_type=F32)


def _dot_nt(a, b):
    return lax.dot_general(a, b, (((1,), (1,)), ((), ())), preferred_element_type=F32)


def _rms(x, g):
    ms = jnp.mean(x * x, axis=-1, keepdims=True)
    return x * lax.rsqrt(ms + RMS_EPS) * g


def _softplus(z):
    return jnp.maximum(z, 0.0) + jnp.log(1.0 + jnp.exp(-jnp.abs(z)))


def _sigmoid(z):
    return 1.0 / (1.0 + jnp.exp(-z))


def _split_bf16(x):
    hi = x.astype(BF16)
    lo = (x - hi.astype(F32)).astype(BF16)
    return hi, lo


def _proj_kernel(x_ref, c_ref, s_ref, gpre_ref, w_ref, gq_ref, wuq_ref, wuk_ref, gkv_ref,
                 qsb_ref, k_ref, v_ref, kbf_ref, vbf_ref, sza_ref, szb_ref, ga_ref, gb_ref,
                 qlat_ref, qrope_ref, ckv_ref, kcat_ref, krope_ref):
    h = _rms(x_ref[...], gpre_ref[...]).astype(BF16)

    def proj(a, b):
        return _dot(h, w_ref[:, a:b])

    qsb_ref[...] = (proj(_C_Q, _C_K) * SB_SCALE).astype(BF16)
    r = proj(_C_K, _C_V)
    k_ref[...] = r
    kbf_ref[...] = r.astype(BF16)
    r = proj(_C_V, _C_ZA)
    v_ref[...] = r
    vbf_ref[...] = r.astype(BF16)
    r = proj(_C_ZA, _C_CQ)
    sza_ref[...] = r * _sigmoid(r)
    r = proj(_C_ZB, _C_GA)
    szb_ref[...] = r * _sigmoid(r)
    ga_ref[...] = _sigmoid(proj(_C_GA, _C_GB))
    gb_ref[...] = _sigmoid(proj(_C_GB, _C_KR))

    cos = c_ref[...]
    sin = s_ref[...]

    ckv = _rms(proj(_C_CKV, _C_ZB), gkv_ref[...])
    ckv_ref[...] = ckv
    kcat_ref[:, 0:KV_LORA] = ckv.astype(BF16)
    kr = proj(_C_KR, _C_KRS) * cos + proj(_C_KRS, _N_AUG) * sin
    krope_ref[...] = kr[:, 0:MLA_ROPE]
    kcat_ref[:, KV_LORA:KV_LORA + ROPE_TILED] = kr.astype(BF16)

    cq = _rms(proj(_C_CQ, _C_CKV), gq_ref[...]).astype(BF16)
    qm = _dot(cq, wuq_ref[...])
    n_nope = MLA_HEADS * NOPE_PAD
    for hd in range(MLA_HEADS):
        qn = qm[:, hd * NOPE_PAD:(hd + 1) * NOPE_PAD].astype(BF16)
        ql = _dot(qn, wuk_ref[hd]) * MLA_SCALE
        qlat_ref[:, hd * KV_LORA:(hd + 1) * KV_LORA] = ql.astype(BF16)
    qr = qm[:, n_nope:n_nope + ROPE_TILED] * cos + qm[:, n_nope + ROPE_TILED:] * sin
    qrope_ref[...] = (qr * MLA_SCALE).astype(BF16)


def _run_proj(x2d, cos_t, sin_t, n_table_blocks, gpre, w_aug, gq, wuq_aug, wuk_pad, gkv, tm):
    rows = x2d.shape[0]
    grid = (rows // tm,)
    row = lambda t: (t, 0)
    tab = lambda t: (t % n_table_blocks, 0)
    const2 = lambda t: (0, 0)
    const3 = lambda t: (0, 0, 0)

    def out(width, dtype):
        return jax.ShapeDtypeStruct((rows, width), dtype), pl.BlockSpec((tm, width), row)

    outs = [out(SB_WIDTH, BF16), out(SB_WIDTH, F32), out(SB_WIDTH, F32), out(SB_WIDTH, BF16),
            out(SB_WIDTH, BF16), out(SB_WIDTH, F32), out(MLA_WIDTH, F32), out(D_MODEL, F32),
            out(D_MODEL, F32), out(MLA_HEADS * KV_LORA, BF16), out(ROPE_TILED, BF16),
            out(KV_LORA, F32), out(KV_LORA + ROPE_TILED, BF16), out(MLA_ROPE, F32)]
    return pl.pallas_call(
        _proj_kernel,
        grid=grid,
        in_specs=[
            pl.BlockSpec((tm, D_MODEL), row),
            pl.BlockSpec((tm, ROPE_TILED), tab),
            pl.BlockSpec((tm, ROPE_TILED), tab),
            pl.BlockSpec((1, D_MODEL), const2),
            pl.BlockSpec(w_aug.shape, const2),
            pl.BlockSpec((1, Q_LORA), const2),
            pl.BlockSpec(wuq_aug.shape, const2),
            pl.BlockSpec(wuk_pad.shape, const3),
            pl.BlockSpec((1, KV_LORA), const2),
        ],
        out_specs=[o[1] for o in outs],
        out_shape=[o[0] for o in outs],
        compiler_params=pltpu.CompilerParams(
            dimension_semantics=("arbitrary",), vmem_limit_bytes=VMEM_LIMIT),
        name="proj",
    )(x2d, cos_t, sin_t, gpre, w_aug, gq, wuq_aug, wuk_pad, gkv)


def _sb_block(q, kb, vb, tri, mask, carry, acc):
    z = _dot_nt(q, kb)
    sp = _softplus(z)
    lk = -sp if mask is None else jnp.where(mask, -sp, 0.0)
    hi, lo = _split_bf16(lk)
    later = _dot(hi, tri) + _dot(lo, tri) + carry
    w = jnp.exp((z - sp) + later)
    if mask is not None:
        w = jnp.where(mask, w, 0.0)
    acc = acc + _dot(w.astype(BF16), vb)
    carry = carry + jnp.sum(lk, axis=-1, keepdims=True)
    return carry, acc


def _sb_prompt_kernel(q_ref, k_ref, v_ref, km_ref, vm_ref, tri_ref, o_ref):
    qi = pl.program_id(2)
    t = SB_T
    q = q_ref[...]
    tri = tri_ref[...]
    rows = lax.broadcasted_iota(jnp.int32, (t, t), 0)
    cols = lax.broadcasted_iota(jnp.int32, (t, t), 1)

    carry = jnp.zeros((t, 1), F32)
    acc = jnp.zeros((t, SB_HEAD_DIM), F32)

    start = pl.multiple_of(qi * t, t)
    carry, acc = _sb_block(q, k_ref[pl.ds(start, t), :], v_ref[pl.ds(start, t), :], tri,
                           cols < rows, carry, acc)

    def body(j, state):
        s0 = pl.multiple_of((qi - 1 - j) * t, t)
        return _sb_block(q, k_ref[pl.ds(s0, t), :], v_ref[pl.ds(s0, t), :], tri, None, *state)

    carry, acc = lax.fori_loop(0, qi, body, (carry, acc))

    tm = km_ref.shape[0]
    mrows = lax.broadcasted_iota(jnp.int32, (t, tm), 1)
    carry, acc = _sb_block(q, km_ref[...], vm_ref[...], tri[:tm, :tm], mrows < N_META, carry, acc)
    o_ref[...] = acc


def _run_sb_prompt(q_hm, k_hm, v_hm, k_meta, v_meta, tri):
    b, hds, seq, d = q_hm.shape
    t = SB_T
    tm = k_meta.shape[1]
    return pl.pallas_call(
        _sb_prompt_kernel,
        grid=(b, hds, seq // t),
        in_specs=[
            pl.BlockSpec((None, None, t, d), lambda i, j, k: (i, j, k, 0)),
            pl.BlockSpec((None, None, seq, d), lambda i, j, k: (i, j, 0, 0)),
            pl.BlockSpec((None, None, seq, d), lambda i, j, k: (i, j, 0, 0)),
            pl.BlockSpec((None, tm, d), lambda i, j, k: (j, 0, 0)),
            pl.BlockSpec((None, tm, d), lambda i, j, k: (j, 0, 0)),
            pl.BlockSpec(tri.shape, lambda i, j, k: (0, 0)),
        ],
        out_specs=pl.BlockSpec((None, None, t, d), lambda i, j, k: (i, j, k, 0)),
        out_shape=jax.ShapeDtypeStruct((b, hds, seq, d), F32),
        compiler_params=pltpu.CompilerParams(
            dimension_semantics=("arbitrary", "arbitrary", "arbitrary"),
            vmem_limit_bytes=VMEM_LIMIT),
        name="sb_prompt",
    )(q_hm, k_hm, v_hm, k_meta, v_meta, tri)


def _mla_block(qcat, kc, mask, m, l, acc):
    s = _dot_nt(qcat, kc)
    if mask is not None:
        s = jnp.where(mask, s, NEG_INF)
    m_new = jnp.maximum(m, jnp.max(s, axis=-1, keepdims=True))
    alpha = jnp.exp(m - m_new)
    p = jnp.exp(s - m_new)
    l = alpha * l + jnp.sum(p, axis=-1, keepdims=True)
    acc = alpha * acc + _dot(p.astype(BF16), kc[:, 0:KV_LORA])
    return m_new, l, acc


def _mla_prompt_kernel(qlat_ref, qrope_ref, kcat_ref, kmeta_ref, wuv_ref, o_ref,
                       qcat_ref, m_ref, l_ref, acc_ref):
    qi = pl.program_id(1)
    tq, tk = MLA_TQ, MLA_TK
    rows_all = MLA_HEADS * tq

    lane = lax.broadcasted_iota(jnp.int32, (tq, ROPE_TILED), 1)
    qrope = qrope_ref[...]
    for hd in range(MLA_HEADS):
        qcat_ref[hd * tq:(hd + 1) * tq, 0:KV_LORA] = qlat_ref[:, hd * KV_LORA:(hd + 1) * KV_LORA]
        own = (lane >= hd * MLA_ROPE) & (lane < (hd + 1) * MLA_ROPE)
        qcat_ref[hd * tq:(hd + 1) * tq, KV_LORA:] = jnp.where(own, qrope, jnp.zeros_like(qrope))
    qcat = qcat_ref[...]

    m_ref[...] = jnp.full((rows_all, 1), NEG_INF, F32)
    l_ref[...] = jnp.zeros((rows_all, 1), F32)
    acc_ref[...] = jnp.zeros((rows_all, KV_LORA), F32)

    def step(kc, mask):
        m, l, acc = _mla_block(qcat, kc, mask, m_ref[...], l_ref[...], acc_ref[...])
        m_ref[...] = m
        l_ref[...] = l
        acc_ref[...] = acc

    jd = qi // 2
    r_in = lax.broadcasted_iota(jnp.int32, (rows_all, tk), 0) & (tq - 1)
    c_in = lax.broadcasted_iota(jnp.int32, (rows_all, tk), 1)
    off = (qi - 2 * jd) * tq
    step(kcat_ref[pl.ds(pl.multiple_of(jd * tk, tk), tk), :], (c_in - r_in) <= off)

    def body(j, _):
        step(kcat_ref[pl.ds(pl.multiple_of(j * tk, tk), tk), :], None)
        return 0

    lax.fori_loop(0, jd, body, 0)

    tmeta = kmeta_ref.shape[0]
    mcols = lax.broadcasted_iota(jnp.int32, (rows_all, tmeta), 1)
    step(kmeta_ref[...], mcols < N_META)

    o_lat = (acc_ref[...] / l_ref[...]).astype(BF16)
    full = _dot(o_lat, wuv_ref[...])
    col_head = lax.broadcasted_iota(jnp.int32, (tq, MLA_WIDTH), 1) // MLA_V
    out = jnp.zeros((tq, MLA_WIDTH), F32)
    for hd in range(MLA_HEADS):
        out = out + jnp.where(col_head == hd, full[hd * tq:(hd + 1) * tq, :], 0.0)
    o_ref[...] = out


def _run_mla_prompt(qlat, qrope, kcat, kcat_meta, wuv_all, batch, seq):
    tq = MLA_TQ
    nq = seq // tq
    kcat3 = kcat.reshape(batch, seq, kcat.shape[-1])
    rows_all = MLA_HEADS * tq
    return pl.pallas_call(
        _mla_prompt_kernel,
        grid=(batch, nq),
        in_specs=[
            pl.BlockSpec((tq, qlat.shape[1]), lambda i, j: (i * nq + j, 0)),
            pl.BlockSpec((tq, qrope.shape[1]), lambda i, j: (i * nq + j, 0)),
            pl.BlockSpec((None, seq, kcat3.shape[-1]), lambda i, j: (i, 0, 0)),
            pl.BlockSpec(kcat_meta.shape, lambda i, j: (0, 0)),
            pl.BlockSpec(wuv_all.shape, lambda i, j: (0, 0)),
        ],
        out_specs=pl.BlockSpec((tq, MLA_WIDTH), lambda i, j: (i * nq + j, 0)),
        out_shape=jax.ShapeDtypeStruct((batch * seq, MLA_WIDTH), F32),
        scratch_shapes=[
            pltpu.VMEM((rows_all, KV_LORA + ROPE_TILED), BF16),
            pltpu.VMEM((rows_all, 1), F32),
            pltpu.VMEM((rows_all, 1), F32),
            pltpu.VMEM((rows_all, KV_LORA), F32),
        ],
        compiler_params=pltpu.CompilerParams(
            dimension_semantics=("arbitrary", "arbitrary"), vmem_limit_bytes=VMEM_LIMIT),
        name="mla_prompt",
    )(qlat, qrope, kcat3, kcat_meta, wuv_all)


def _rev_cumsum_blocks(lk, tri):
    rows, width = lk.shape
    cb = tri.shape[0]
    nb = width // cb
    blocks = [lk[:, b * cb:(b + 1) * cb] for b in range(nb)]
    stacked = blocks[0] if nb == 1 else jnp.concatenate(blocks, axis=0)
    hi, lo = _split_bf16(stacked)
    loc = _dot(hi, tri) + _dot(lo, tri)
    tots = jnp.sum(stacked, axis=-1, keepdims=True)
    outs = [None] * nb
    off = jnp.zeros((rows, 1), F32)
    for b in range(nb - 1, -1, -1):
        outs[b] = loc[b * rows:(b + 1) * rows, :] + off
        off = off + tots[b * rows:(b + 1) * rows, :]
    later = outs[0] if nb == 1 else jnp.concatenate(outs, axis=1)
    return later, off


def _decode_chunk(qbd, qlat, qrope, kc, vc, cc, rc, tri, sb_mask, mla_mask, state):
    carry, acc_a, m, l, acc_b = state
    z = _dot_nt(qbd, kc)
    sp = _softplus(z)
    lk = -sp if sb_mask is None else jnp.where(sb_mask, -sp, 0.0)
    later, total = _rev_cumsum_blocks(lk, tri)
    w = jnp.exp((z - sp) + (later + carry))
    if sb_mask is not None:
        w = jnp.where(sb_mask, w, 0.0)
    acc_a = acc_a + _dot(w.astype(BF16), vc)
    carry = carry + total
    s = _dot_nt(qlat, cc) + _dot_nt(qrope, rc)
    if mla_mask is not None:
        s = jnp.where(mla_mask, s, NEG_INF)
    m_new = jnp.maximum(m, jnp.max(s, axis=-1, keepdims=True))
    alpha = jnp.exp(m - m_new)
    p = jnp.exp(s - m_new)
    l = alpha * l + jnp.sum(p, axis=-1, keepdims=True)
    acc_b = alpha * acc_b + _dot(p.astype(BF16), cc)
    return carry, acc_a, m_new, l, acc_b


def _head_diag_rows(full):
    rows, width = full.shape
    nq = rows // SB_HEADS
    row_head = lax.broadcasted_iota(jnp.int32, (rows, width), 0) // nq
    col_head = lax.broadcasted_iota(jnp.int32, (rows, width), 1) // SB_HEAD_DIM
    kept = jnp.where(row_head == col_head, full, 0.0)
    out = kept[0:nq, :]
    for hd in range(1, SB_HEADS):
        out = out + kept[hd * nq:(hd + 1) * nq, :]
    return out


def _decode_kernel(pt_ref, qbd_ref, qlat_ref, qrope_ref, knew_ref, vnew_ref, cnew_ref, rnew_ref,
                   tri_ref, wuv_ref, ck_hbm, cv_hbm, cc_hbm, cr_hbm, oa_ref, ob_ref,
                   kbuf, vbuf, cbuf, rbuf, sem):
    s = pl.program_id(0)
    n_seq = pl.num_programs(0)
    n_pages = pt_ref.shape[1]
    g = PAGES_PER_CHUNK
    n_chunks = n_pages // g
    page = kbuf.shape[1] // g

    def copies(seq, chunk, slot):
        out = []
        for p in range(g):
            pg = pt_ref[seq, chunk * g + p]
            dst = pl.ds(p * page, page)
            out.append(pltpu.make_async_copy(ck_hbm.at[pg], kbuf.at[slot, dst], sem.at[slot]))
            out.append(pltpu.make_async_copy(cv_hbm.at[pg], vbuf.at[slot, dst], sem.at[slot]))
            out.append(pltpu.make_async_copy(cc_hbm.at[pg], cbuf.at[slot, dst], sem.at[slot]))
            out.append(pltpu.make_async_copy(cr_hbm.at[pg], rbuf.at[slot, dst], sem.at[slot]))
        return out

    def start_chunk(seq, chunk, slot):
        for c in copies(seq, chunk, slot):
            c.start()

    def wait_chunk(seq, chunk, slot):
        for c in copies(seq, chunk, slot):
            c.wait()

    @pl.when(s == 0)
    def _():
        start_chunk(0, n_chunks - 1, 0)

    qbd = qbd_ref[...]
    qlat = qlat_ref[...]
    qrope = qrope_ref[...]
    tri = tri_ref[...]
    rows = qbd.shape[0]
    nq = rows // SB_HEADS

    state = (jnp.zeros((rows, 1), F32), jnp.zeros((rows, SB_WIDTH), F32),
             jnp.full((rows, 1), NEG_INF, F32), jnp.zeros((rows, 1), F32),
             jnp.zeros((rows, KV_LORA), F32))

    n_new = knew_ref.shape[0]

    def pad_new(ref):
        x = ref[...]
        return jnp.concatenate([x, jnp.zeros((page - n_new, x.shape[1]), x.dtype)], axis=0).astype(BF16)

    q_idx = lax.broadcasted_iota(jnp.int32, (rows, page), 0) & (nq - 1)
    k_idx = lax.broadcasted_iota(jnp.int32, (rows, page), 1)
    state = _decode_chunk(qbd, qlat, qrope, pad_new(knew_ref), pad_new(vnew_ref), pad_new(cnew_ref),
                          pad_new(rnew_ref), tri_ref[0:page, 0:page], k_idx < q_idx, k_idx <= q_idx,
                          state)

    def body(k, state):
        chunk = n_chunks - 1 - k
        slot = k & 1

        @pl.when(k < n_chunks - 1)
        def _():
            start_chunk(s, chunk - 1, 1 - slot)

        @pl.when(jnp.logical_and(k == n_chunks - 1, s + 1 < n_seq))
        def _():
            start_chunk(s + 1, n_chunks - 1, 1 - slot)

        wait_chunk(s, chunk, slot)
        return _decode_chunk(qbd, qlat, qrope, kbuf[slot].astype(BF16), vbuf[slot].astype(BF16),
                             cbuf[slot].astype(BF16), rbuf[slot].astype(BF16), tri, None, None, state)

    carry, acc_a, m, l, acc_b = lax.fori_loop(0, n_chunks, body, state)

    oa_ref[...] = _head_diag_rows(acc_a)
    o_lat = (acc_b / l).astype(BF16)
    ob_ref[...] = _head_diag_rows(_dot(o_lat, wuv_ref[...]))


def _run_decode(page_table, qbd, qlat, qrope, knew, vnew, cnew, rnew, tri, wuv_all,
                cache_k, cache_v, cache_c, cache_r):
    n_seq, n_pages = page_table.shape
    page = cache_k.shape[1]
    rows = qbd.shape[1]
    nq = rows // SB_HEADS
    n_new = knew.shape[1]
    tok = PAGES_PER_CHUNK * page
    per_seq = lambda i, pt: (i, 0, 0)
    const2 = lambda i, pt: (0, 0)
    grid_spec = pltpu.PrefetchScalarGridSpec(
        num_scalar_prefetch=1,
        grid=(n_seq,),
        in_specs=[
            pl.BlockSpec((None, rows, SB_WIDTH), per_seq),
            pl.BlockSpec((None, rows, KV_LORA), per_seq),
            pl.BlockSpec((None, rows, MLA_ROPE), per_seq),
            pl.BlockSpec((None, n_new, SB_WIDTH), per_seq),
            pl.BlockSpec((None, n_new, SB_WIDTH), per_seq),
            pl.BlockSpec((None, n_new, KV_LORA), per_seq),
            pl.BlockSpec((None, n_new, MLA_ROPE), per_seq),
            pl.BlockSpec(tri.shape, const2),
            pl.BlockSpec(wuv_all.shape, const2),
            pl.BlockSpec(memory_space=pl.ANY),
            pl.BlockSpec(memory_space=pl.ANY),
            pl.BlockSpec(memory_space=pl.ANY),
            pl.BlockSpec(memory_space=pl.ANY),
        ],
        out_specs=[
            pl.BlockSpec((None, nq, SB_WIDTH), per_seq),
            pl.BlockSpec((None, nq, MLA_WIDTH), per_seq),
        ],
        scratch_shapes=[
            pltpu.VMEM((2, tok, SB_WIDTH), F32),
            pltpu.VMEM((2, tok, SB_WIDTH), F32),
            pltpu.VMEM((2, tok, KV_LORA), F32),
            pltpu.VMEM((2, tok, MLA_ROPE), F32),
            pltpu.SemaphoreType.DMA((2,)),
        ],
    )
    return pl.pallas_call(
        _decode_kernel,
        grid_spec=grid_spec,
        out_shape=[jax.ShapeDtypeStruct((n_seq, nq, SB_WIDTH), F32),
                   jax.ShapeDtypeStruct((n_seq, nq, MLA_WIDTH), F32)],
        compiler_params=pltpu.CompilerParams(
            dimension_semantics=("arbitrary",), vmem_limit_bytes=VMEM_LIMIT),
        name="decode",
    )(page_table, qbd, qlat, qrope, knew, vnew, cnew, rnew, tri, wuv_all,
      cache_k, cache_v, cache_c, cache_r)


def _out_kernel(x_ref, oa_ref, ob_ref, sza_ref, szb_ref, ga_ref, gb_ref, wpa_ref, wpb_ref,
                wout_ref, gpost_ref, y_ref):
    a = (oa_ref[...] * sza_ref[...]).astype(BF16)
    b = (ob_ref[...] * szb_ref[...]).astype(BF16)
    merged = ga_ref[...] * _dot(a, wpa_ref[...]) + gb_ref[...] * _dot(b, wpb_ref[...])
    t = _dot(merged.astype(BF16), wout_ref[...])
    y_ref[...] = x_ref[...] + _rms(t, gpost_ref[...])


def _run_out(x2d, oa, ob, sza, szb, ga, gb, wpa, wpb, wout, gpost, tm):
    rows = x2d.shape[0]
    row = lambda t: (t, 0)
    const2 = lambda t: (0, 0)
    return pl.pallas_call(
        _out_kernel,
        grid=(rows // tm,),
        in_specs=[
            pl.BlockSpec((tm, D_MODEL), row),
            pl.BlockSpec((tm, SB_WIDTH), row),
            pl.BlockSpec((tm, MLA_WIDTH), row),
            pl.BlockSpec((tm, SB_WIDTH), row),
            pl.BlockSpec((tm, MLA_WIDTH), row),
            pl.BlockSpec((tm, D_MODEL), row),
            pl.BlockSpec((tm, D_MODEL), row),
            pl.BlockSpec(wpa.shape, const2),
            pl.BlockSpec(wpb.shape, const2),
            pl.BlockSpec(wout.shape, const2),
            pl.BlockSpec((1, D_MODEL), const2),
        ],
        out_specs=pl.BlockSpec((tm, D_MODEL), row),
        out_shape=jax.ShapeDtypeStruct((rows, D_MODEL), F32),
        compiler_params=pltpu.CompilerParams(
            dimension_semantics=("arbitrary",), vmem_limit_bytes=VMEM_LIMIT),
        name="out_mix",
    )(x2d, oa, ob, sza, szb, ga, gb, wpa, wpb, wout, gpost)


def _rope_tables(pos):
    half = MLA_ROPE // 2
    inv = ROPE_THETA ** (-jnp.arange(half, dtype=jnp.float32) * 2.0 / MLA_ROPE)
    ang = pos.astype(jnp.float32)[:, None] * inv[None, :]
    cos, sin = jnp.cos(ang), jnp.sin(ang)
    cos_t = jnp.repeat(cos, 2, axis=1)
    sin_t = jnp.stack([-sin, sin], axis=-1).reshape(pos.shape[0], MLA_ROPE)
    return jnp.tile(cos_t, (1, MLA_HEADS)), jnp.tile(sin_t, (1, MLA_HEADS))


def kernel(x_prompt, x_sample, cache_sb_k, cache_sb_v, cache_mla_ckv, cache_mla_krope, page_table,
           meta_tokens, g_pre, w_in, g_qnorm, w_uq, g_kvnorm, w_uk, w_uv, w_proj_a, w_proj_b,
           w_out, g_post):
    assert g_pre.shape[0] == 1, "single layer step"
    batch, seq, d_model = x_prompt.shape
    n_dec, dec_seq, _ = x_sample.shape
    n_pool, page = cache_sb_k.shape[1], cache_sb_k.shape[2]
    past_len = page_table.shape[1] * page
    pair_swap = jnp.arange(MLA_ROPE) ^ 1

    w = w_in[0]
    o_kr = 4 * SB_WIDTH + Q_LORA + KV_LORA
    w_kr = w[:, o_kr:o_kr + MLA_ROPE]
    w_aug = jnp.concatenate(
        [w[:, :o_kr], w[:, o_kr + MLA_ROPE:], jnp.tile(w_kr, (1, MLA_HEADS)),
         jnp.tile(w_kr[:, pair_swap], (1, MLA_HEADS))], axis=1).astype(BF16)
    assert w_aug.shape[1] == _N_AUG
    wq = w_uq[0].reshape(Q_LORA, MLA_HEADS, MLA_NOPE + MLA_ROPE)
    wq_nope = jnp.pad(wq[:, :, :MLA_NOPE], ((0, 0), (0, 0), (0, NOPE_PAD - MLA_NOPE)))
    wq_rope = wq[:, :, MLA_NOPE:]
    wuq_aug = jnp.concatenate(
        [wq_nope.reshape(Q_LORA, -1), wq_rope.reshape(Q_LORA, -1),
         wq_rope[:, :, pair_swap].reshape(Q_LORA, -1)], axis=1).astype(BF16)
    wuk_pad = jnp.pad(jnp.swapaxes(w_uk[0], 1, 2),
                      ((0, 0), (0, NOPE_PAD - MLA_NOPE), (0, 0))).astype(BF16)
    wuv_all = jnp.swapaxes(w_uv[0], 0, 1).reshape(KV_LORA, MLA_WIDTH).astype(BF16)
    wpa = w_proj_a[0].astype(BF16)
    wpb = w_proj_b[0].astype(BF16)
    wout = w_out[0].astype(BF16)
    idx = jnp.arange(CUM_B)
    tri = (idx[:, None] > idx[None, :]).astype(BF16)

    xp2 = x_prompt.reshape(batch * seq, d_model)
    cos_p, sin_p = _rope_tables(N_META + jnp.arange(seq, dtype=jnp.int32))
    proj_p = _run_proj(xp2, cos_p, sin_p, seq // PROJ_TM, g_pre, w_aug, g_qnorm, wuq_aug, wuk_pad,
                       g_kvnorm, PROJ_TM)
    n_s = n_dec * dec_seq
    xs2 = x_sample.reshape(n_s, d_model)
    x_small = jnp.concatenate([xs2, meta_tokens.astype(x_prompt.dtype)], axis=0)
    pos_small = jnp.concatenate([past_len + (jnp.arange(n_s, dtype=jnp.int32) % dec_seq),
                                 jnp.arange(N_META, dtype=jnp.int32)])
    cos_s, sin_s = _rope_tables(pos_small)
    n_small = n_s + N_META
    small_tm = n_small // 3
    proj_s = _run_proj(x_small, cos_s, sin_s, n_small // small_tm, g_pre, w_aug, g_qnorm, wuq_aug,
                       wuk_pad, g_kvnorm, small_tm)

    (qsb_p, k_p, v_p, kbf_p, vbf_p, sza_p, szb_p, ga_p, gb_p, qlat_p, qrope_p, ckv_p, kcat_p,
     krope_p) = proj_p
    (qsb_s, k_s, v_s, kbf_s, vbf_s, sza_s, szb_s, ga_s, gb_s, qlat_s, qrope_s, ckv_s, kcat_s,
     krope_s) = proj_s

    def head_major(a):
        return a.reshape(batch, seq, SB_HEADS, SB_HEAD_DIM).transpose(0, 2, 1, 3)

    def meta_head_major(a):
        m = a[n_s:].reshape(N_META, SB_HEADS, SB_HEAD_DIM).transpose(1, 0, 2)
        return jnp.pad(m, ((0, 0), (0, 128 - N_META), (0, 0)))

    oa_hm = _run_sb_prompt(head_major(qsb_p), head_major(kbf_p), head_major(vbf_p),
                           meta_head_major(kbf_s), meta_head_major(vbf_s), tri)
    oa_p = oa_hm.transpose(0, 2, 1, 3).reshape(batch * seq, SB_WIDTH)
    kcat_meta = jnp.pad(kcat_s[n_s:], ((0, 128 - N_META), (0, 0)))
    ob_p = _run_mla_prompt(qlat_p, qrope_p, kcat_p, kcat_meta, wuv_all, batch, seq)
    y_p = _run_out(xp2, oa_p, ob_p, sza_p, szb_p, ga_p, gb_p, wpa, wpb, wout, g_post, OUT_TM)

    rows = SB_HEADS * dec_seq
    q4 = qsb_s[:n_s].reshape(n_dec, dec_seq, SB_WIDTH)
    qbd = jnp.tile(q4, (1, SB_HEADS, 1)).reshape(n_dec, SB_HEADS, dec_seq, SB_WIDTH)
    own = (jnp.arange(SB_WIDTH)[None, :] // SB_HEAD_DIM) == jnp.arange(SB_HEADS)[:, None]
    qbd = jnp.where(own[None, :, None, :], qbd, jnp.zeros_like(qbd)).reshape(n_dec, rows, SB_WIDTH)
    qlat_d = qlat_s[:n_s].reshape(n_dec, dec_seq, MLA_HEADS, KV_LORA).transpose(0, 2, 1, 3)
    qlat_d = qlat_d.reshape(n_dec, rows, KV_LORA)
    qrope_d = qrope_s[:n_s].reshape(n_dec, dec_seq, MLA_HEADS, MLA_ROPE).transpose(0, 2, 1, 3)
    qrope_d = qrope_d.reshape(n_dec, rows, MLA_ROPE)

    def new_rows(a):
        a = a[:n_s].reshape(n_dec, dec_seq, a.shape[-1])
        return jnp.pad(a, ((0, 0), (0, 8 - dec_seq), (0, 0)))

    oa_s, ob_s = _run_decode(
        page_table, qbd, qlat_d, qrope_d, new_rows(k_s), new_rows(v_s), new_rows(ckv_s),
        new_rows(krope_s), tri, wuv_all,
        cache_sb_k[0].reshape(n_pool, page, SB_WIDTH), cache_sb_v[0].reshape(n_pool, page, SB_WIDTH),
        cache_mla_ckv[0], cache_mla_krope[0])
    y_s = _run_out(xs2, oa_s.reshape(n_s, SB_WIDTH), ob_s.reshape(n_s, MLA_WIDTH), sza_s[:n_s],
                   szb_s[:n_s], ga_s[:n_s], gb_s[:n_s], wpa, wpb, wout, g_post, n_s)

    def with_meta(small, big):
        width = big.shape[-1]
        meta = jnp.broadcast_to(small[n_s:][None], (batch, N_META, width))
        return jnp.concatenate([meta, big.reshape(batch, seq, width)], axis=1)[None]

    lp = seq + N_META
    return (
        y_p.reshape(batch, seq, d_model),
        y_s.reshape(n_dec, dec_seq, d_model),
        with_meta(k_s, k_p).reshape(1, batch, lp, SB_HEADS, SB_HEAD_DIM),
        with_meta(v_s, v_p).reshape(1, batch, lp, SB_HEADS, SB_HEAD_DIM),
        with_meta(ckv_s, ckv_p),
        with_meta(krope_s, krope_p),
        k_s[:n_s].reshape(1, n_dec, dec_seq, SB_HEADS, SB_HEAD_DIM),
        v_s[:n_s].reshape(1, n_dec, dec_seq, SB_HEADS, SB_HEAD_DIM),
        ckv_s[:n_s].reshape(1, n_dec, dec_seq, KV_LORA),
        krope_s[:n_s].reshape(1, n_dec, dec_seq, MLA_ROPE),
    )
```

```python
import jax
import jax.numpy as jnp
from jax import lax
from jax.experimental import pallas as pl
from jax.experimental.pallas import tpu as pltpu

F32 = jnp.float32
BF16 = jnp.bfloat16

N_META = 16
SB_HEADS = 8
SB_HEAD_DIM = 64
SB_WIDTH = SB_HEADS * SB_HEAD_DIM
MLA_HEADS = 8
MLA_NOPE = 64
MLA_ROPE = 32
MLA_V = 64
MLA_WIDTH = MLA_HEADS * MLA_V
Q_LORA = 384
KV_LORA = 256
D_MODEL = 1024
MLA_SCALE = (MLA_NOPE + MLA_ROPE) ** -0.5
SB_SCALE = SB_HEAD_DIM ** -0.5
ROPE_THETA = 10000.0
RMS_EPS = 1e-6
NEG_INF = -1e30

LANES = 128
ROPE_TILED = MLA_HEADS * MLA_ROPE
NOPE_PAD = LANES
PAIR = 2 * SB_HEAD_DIM
N_PAIRS = SB_WIDTH // PAIR

_C_Q, _C_K, _C_V, _C_ZA = 0, 512, 1024, 1536
_C_CQ = 2048
_C_CKV = _C_CQ + Q_LORA
_C_ZB = _C_CKV + KV_LORA
_C_GA = _C_ZB + MLA_WIDTH
_C_GB = _C_GA + D_MODEL
_C_KR = _C_GB + D_MODEL
_C_KRS = _C_KR + ROPE_TILED
_N_AUG = _C_KRS + ROPE_TILED
_R_ROPE = MLA_HEADS * NOPE_PAD
_R_ROPES = _R_ROPE + ROPE_TILED
_N_UQ = _R_ROPES + ROPE_TILED

VMEM_LIMIT = 56 * 1024 * 1024

KEY_BLOCK = 256
SMALL_TM = 128
OUT_TM = 512
MLA_TQ = 256
PAGES_PER_CHUNK = 16
N_SLOTS = 3


def _dot(a, b):
    return jnp.dot(a, b, preferred_element_type=F32)


def _dot_nt(a, b):
    return lax.dot_general(a, b, (((1,), (1,)), ((), ())), preferred_element_type=F32)


def _rms(x, g):
    ms = jnp.mean(x * x, axis=-1, keepdims=True)
    return x * lax.rsqrt(ms + RMS_EPS) * g


def _softplus(z):
    return jnp.maximum(z, 0.0) + jnp.log(1.0 + jnp.exp(-jnp.abs(z)))


def _sigmoid(z):
    return 1.0 / (1.0 + jnp.exp(-z))


def _split_bf16(x):
    hi = x.astype(BF16)
    lo = (x - hi.astype(F32)).astype(BF16)
    return hi, lo


def _proj_kernel(x_ref, c_ref, s_ref, ct_ref, st_ref, gpre_ref, w_ref, wkt_ref, wct_ref, gq_ref,
                 wuqt_ref, wuk_ref, gkv_ref, gkvt_ref,
                 qsb_ref, k_ref, kt_ref, v_ref, vbf_ref, sza_ref, szb_ref, ga_ref, gb_ref,
                 qlatt_ref, qropet_ref, ckv_ref, kcat_ref, ckvt_ref, krope_ref):
    h = _rms(x_ref[...], gpre_ref[...]).astype(BF16)

    def proj(a, b):
        return _dot(h, w_ref[:, a:b])

    qsb_ref[...] = (proj(_C_Q, _C_K) * SB_SCALE).astype(BF16)
    k_ref[...] = proj(_C_K, _C_V)
    kt_ref[...] = _dot_nt(wkt_ref[...], h).astype(BF16)
    r = proj(_C_V, _C_ZA)
    v_ref[...] = r
    vbf_ref[...] = r.astype(BF16)
    r = proj(_C_ZA, _C_CQ)
    sza_ref[...] = r * _sigmoid(r)
    r = proj(_C_ZB, _C_GA)
    szb_ref[...] = r * _sigmoid(r)
    ga_ref[...] = _sigmoid(proj(_C_GA, _C_GB))
    gb_ref[...] = _sigmoid(proj(_C_GB, _C_KR))

    ckv = _rms(proj(_C_CKV, _C_ZB), gkv_ref[...])
    ckv_ref[...] = ckv
    kcat_ref[:, 0:KV_LORA] = ckv.astype(BF16)
    ct = _dot_nt(wct_ref[...], h)
    ms = jnp.mean(ct * ct, axis=0, keepdims=True)
    ckvt_ref[...] = (ct * lax.rsqrt(ms + RMS_EPS) * gkvt_ref[...]).astype(BF16)
    kr = proj(_C_KR, _C_KRS) * c_ref[...] + proj(_C_KRS, _N_AUG) * s_ref[...]
    krope_ref[...] = kr[:, 0:MLA_ROPE]
    kcat_ref[:, KV_LORA:KV_LORA + ROPE_TILED] = kr.astype(BF16)

    cq = _rms(proj(_C_CQ, _C_CKV), gq_ref[...]).astype(BF16)
    qmt = _dot_nt(wuqt_ref[...], cq)
    for hd in range(MLA_HEADS):
        qn = qmt[hd * NOPE_PAD:(hd + 1) * NOPE_PAD, :].astype(BF16)
        ql = _dot(wuk_ref[hd], qn) * MLA_SCALE
        qlatt_ref[hd * KV_LORA:(hd + 1) * KV_LORA, :] = ql.astype(BF16)
    qr = qmt[_R_ROPE:_R_ROPES, :] * ct_ref[...] + qmt[_R_ROPES:_N_UQ, :] * st_ref[...]
    qropet_ref[...] = (qr * MLA_SCALE).astype(BF16)


def _run_proj(x2d, cos_t, sin_t, n_table_blocks, weights, tm):
    gpre, w_aug, wkt, wct, gq, wuqt, wuk_pad, gkv, gkvt = weights
    rows = x2d.shape[0]
    nblk = rows // tm
    row = lambda t: (t, 0)
    col = lambda t: (0, t)
    tab = lambda t: (t % n_table_blocks, 0)
    tabt = lambda t: (0, t % n_table_blocks)
    const2 = lambda t: (0, 0)
    const3 = lambda t: (0, 0, 0)

    def out(width, dtype):
        return jax.ShapeDtypeStruct((rows, width), dtype), pl.BlockSpec((tm, width), row)

    def out_t(height, dtype):
        return jax.ShapeDtypeStruct((height, rows), dtype), pl.BlockSpec((height, tm), col)

    def out_blocked(height, dtype):
        return (jax.ShapeDtypeStruct((nblk, height, tm), dtype),
                pl.BlockSpec((None, height, tm), lambda t: (t, 0, 0)))

    outs = [out(SB_WIDTH, BF16), out(SB_WIDTH, F32), out_blocked(SB_WIDTH, BF16), out(SB_WIDTH, F32),
            out(SB_WIDTH, BF16), out(SB_WIDTH, F32), out(MLA_WIDTH, F32), out(D_MODEL, F32),
            out(D_MODEL, F32), out_t(MLA_HEADS * KV_LORA, BF16), out_t(ROPE_TILED, BF16),
            out(KV_LORA, F32), out(KV_LORA + ROPE_TILED, BF16), out_blocked(KV_LORA, BF16),
            out(MLA_ROPE, F32)]
    cos_tt, sin_tt = cos_t.T, sin_t.T
    return pl.pallas_call(
        _proj_kernel,
        grid=(nblk,),
        in_specs=[
            pl.BlockSpec((tm, D_MODEL), row),
            pl.BlockSpec((tm, ROPE_TILED), tab),
            pl.BlockSpec((tm, ROPE_TILED), tab),
            pl.BlockSpec((ROPE_TILED, tm), tabt),
            pl.BlockSpec((ROPE_TILED, tm), tabt),
            pl.BlockSpec((1, D_MODEL), const2),
            pl.BlockSpec(w_aug.shape, const2),
            pl.BlockSpec(wkt.shape, const2),
            pl.BlockSpec(wct.shape, const2),
            pl.BlockSpec((1, Q_LORA), const2),
            pl.BlockSpec(wuqt.shape, const2),
            pl.BlockSpec(wuk_pad.shape, const3),
            pl.BlockSpec((1, KV_LORA), const2),
            pl.BlockSpec((KV_LORA, 1), const2),
        ],
        out_specs=[o[1] for o in outs],
        out_shape=[o[0] for o in outs],
        compiler_params=pltpu.CompilerParams(
            dimension_semantics=("arbitrary",), vmem_limit_bytes=VMEM_LIMIT),
        name="proj",
    )(x2d, cos_t, sin_t, cos_tt, sin_tt, gpre, w_aug, wkt, wct, gq, wuqt, wuk_pad, gkv, gkvt)


def _sb_pair_block(q2, ktb, vb, tri, mask, carry, acc):
    n = q2.shape[0]
    z = _dot(q2, ktb)
    sp = _softplus(z)
    lk = -sp if mask is None else jnp.where(mask, -sp, 0.0)
    hi, lo = _split_bf16(lk)
    loc = _dot(jnp.concatenate([hi, lo], axis=0), tri)
    later = loc[:n] + loc[n:] + carry
    w = jnp.exp((z - sp) + later)
    if mask is not None:
        w = jnp.where(mask, w, 0.0)
    acc = acc + _dot(w.astype(BF16), vb)
    carry = carry + jnp.sum(lk, axis=-1, keepdims=True)
    return carry, acc


def _sb_prompt_kernel(q_ref, kt_ref, v_ref, ktm_ref, vm_ref, tri_ref, o_ref, q2_ref, carry_ref, acc_ref):
    qi = pl.program_id(1)
    t = KEY_BLOCK
    tri = tri_ref[...]
    even = lax.broadcasted_iota(jnp.int32, (t, PAIR), 1) < SB_HEAD_DIM

    for p in range(N_PAIRS):
        qp = q_ref[:, p * PAIR:(p + 1) * PAIR]
        zero = jnp.zeros_like(qp)
        q2_ref[p, 0:t, :] = jnp.where(even, qp, zero)
        q2_ref[p, t:2 * t, :] = jnp.where(even, zero, qp)
    carry_ref[...] = jnp.zeros(carry_ref.shape, F32)
    acc_ref[...] = jnp.zeros(acc_ref.shape, F32)

    def sweep(ktb_of, vb_of, tri_b, mask):
        for p in range(N_PAIRS):
            carry, acc = _sb_pair_block(q2_ref[p], ktb_of(p), vb_of(p), tri_b, mask,
                                        carry_ref[p], acc_ref[p])
            carry_ref[p] = carry
            acc_ref[p] = acc

    def own_block(kb, mask):
        start = pl.multiple_of(kb * t, t)
        sweep(lambda p: kt_ref[kb, p * PAIR:(p + 1) * PAIR, :],
              lambda p: v_ref[pl.ds(start, t), p * PAIR:(p + 1) * PAIR], tri, mask)

    q_in = lax.broadcasted_iota(jnp.int32, (2 * t, t), 0) & (t - 1)
    k_in = lax.broadcasted_iota(jnp.int32, (2 * t, t), 1)
    own_block(qi, k_in < q_in)

    def body(j, _):
        own_block(qi - 1 - j, None)
        return 0

    lax.fori_loop(0, qi, body, 0)

    tm = ktm_ref.shape[1]
    m_in = lax.broadcasted_iota(jnp.int32, (2 * t, tm), 1)
    sweep(lambda p: ktm_ref[p * PAIR:(p + 1) * PAIR, :],
          lambda p: vm_ref[:, p * PAIR:(p + 1) * PAIR], tri_ref[0:tm, 0:tm], m_in < N_META)

    for p in range(N_PAIRS):
        acc = acc_ref[p]
        o_ref[:, p * PAIR:(p + 1) * PAIR] = jnp.where(even, acc[0:t], acc[t:2 * t])


def _run_sb_prompt(qsb, kt3, vbf, kt_meta, v_meta, tri, batch, seq):
    t = KEY_BLOCK
    nq = seq // t
    tm = kt_meta.shape[1]
    return pl.pallas_call(
        _sb_prompt_kernel,
        grid=(batch, nq),
        in_specs=[
            pl.BlockSpec((t, SB_WIDTH), lambda i, j: (i * nq + j, 0)),
            pl.BlockSpec((nq, SB_WIDTH, t), lambda i, j: (i, 0, 0)),
            pl.BlockSpec((seq, SB_WIDTH), lambda i, j: (i, 0)),
            pl.BlockSpec(kt_meta.shape, lambda i, j: (0, 0)),
            pl.BlockSpec(v_meta.shape, lambda i, j: (0, 0)),
            pl.BlockSpec(tri.shape, lambda i, j: (0, 0)),
        ],
        out_specs=pl.BlockSpec((t, SB_WIDTH), lambda i, j: (i * nq + j, 0)),
        out_shape=jax.ShapeDtypeStruct((batch * seq, SB_WIDTH), F32),
        scratch_shapes=[
            pltpu.VMEM((N_PAIRS, 2 * t, PAIR), BF16),
            pltpu.VMEM((N_PAIRS, 2 * t, 1), F32),
            pltpu.VMEM((N_PAIRS, 2 * t, PAIR), F32),
        ],
        compiler_params=pltpu.CompilerParams(
            dimension_semantics=("arbitrary", "arbitrary"), vmem_limit_bytes=VMEM_LIMIT),
        name="sb_prompt",
    )(qsb, kt3, vbf, kt_meta, v_meta, tri)


def _mla_prompt_kernel(qlatt_ref, qropet_ref, kcat_ref, ckvt_ref, kmeta_ref, ctmeta_ref, wuvt_ref,
                       o_ref, qcat_ref, m_ref, l_ref, acc_ref):
    qi = pl.program_id(1)
    tq, tk = MLA_TQ, KEY_BLOCK
    mq = MLA_HEADS * tq

    sub = lax.broadcasted_iota(jnp.int32, (ROPE_TILED, tq), 0)
    qrt = qropet_ref[...]
    for hd in range(MLA_HEADS):
        qcat_ref[0:KV_LORA, hd * tq:(hd + 1) * tq] = qlatt_ref[hd * KV_LORA:(hd + 1) * KV_LORA, :]
        own = (sub >= hd * MLA_ROPE) & (sub < (hd + 1) * MLA_ROPE)
        qcat_ref[KV_LORA:, hd * tq:(hd + 1) * tq] = jnp.where(own, qrt, jnp.zeros_like(qrt))
    qcat = qcat_ref[...]

    m_ref[...] = jnp.full(m_ref.shape, NEG_INF, F32)
    l_ref[...] = jnp.zeros(l_ref.shape, F32)
    acc_ref[...] = jnp.zeros(acc_ref.shape, F32)

    def step(kc, ct, mask):
        s = _dot(kc, qcat)
        if mask is not None:
            s = jnp.where(mask, s, NEG_INF)
        m_old = m_ref[...]
        m_new = jnp.maximum(m_old, jnp.max(s, axis=0, keepdims=True))
        alpha = jnp.exp(m_old - m_new)
        p = jnp.exp(s - m_new)
        l_ref[...] = alpha * l_ref[...] + jnp.sum(p, axis=0, keepdims=True)
        acc_ref[...] = alpha * acc_ref[...] + _dot(ct, p.astype(BF16))
        m_ref[...] = m_new

    jd = (qi * tq) // tk
    r_in = lax.broadcasted_iota(jnp.int32, (tk, mq), 0)
    c_in = lax.broadcasted_iota(jnp.int32, (tk, mq), 1) & (tq - 1)
    off = qi * tq - jd * tk
    step(kcat_ref[pl.ds(pl.multiple_of(jd * tk, tk), tk), :], ckvt_ref[jd], (r_in - c_in) <= off)

    def body(j, _):
        step(kcat_ref[pl.ds(pl.multiple_of(j * tk, tk), tk), :], ckvt_ref[j], None)
        return 0

    lax.fori_loop(0, jd, body, 0)

    tmeta = kmeta_ref.shape[0]
    mrow = lax.broadcasted_iota(jnp.int32, (tmeta, mq), 0)
    step(kmeta_ref[...], ctmeta_ref[...], mrow < N_META)

    o_lat = (acc_ref[...] / l_ref[...]).astype(BF16)
    full = _dot(wuvt_ref[...], o_lat)
    row_head = lax.broadcasted_iota(jnp.int32, (MLA_WIDTH, tq), 0) // MLA_V
    out_t = jnp.zeros((MLA_WIDTH, tq), F32)
    for hd in range(MLA_HEADS):
        out_t = out_t + jnp.where(row_head == hd, full[:, hd * tq:(hd + 1) * tq], 0.0)
    o_ref[...] = out_t.T


def _run_mla_prompt(qlatt, qropet, kcat, ckvt3, kcat_meta, ckvt_meta, wuvt, batch, seq):
    tq, tk = MLA_TQ, KEY_BLOCK
    nq = seq // tq
    nk = seq // tk
    mq = MLA_HEADS * tq
    return pl.pallas_call(
        _mla_prompt_kernel,
        grid=(batch, nq),
        in_specs=[
            pl.BlockSpec((qlatt.shape[0], tq), lambda i, j: (0, i * nq + j)),
            pl.BlockSpec((qropet.shape[0], tq), lambda i, j: (0, i * nq + j)),
            pl.BlockSpec((seq, kcat.shape[1]), lambda i, j: (i, 0)),
            pl.BlockSpec((nk, KV_LORA, tk), lambda i, j: (i, 0, 0)),
            pl.BlockSpec(kcat_meta.shape, lambda i, j: (0, 0)),
            pl.BlockSpec(ckvt_meta.shape, lambda i, j: (0, 0)),
            pl.BlockSpec(wuvt.shape, lambda i, j: (0, 0)),
        ],
        out_specs=pl.BlockSpec((tq, MLA_WIDTH), lambda i, j: (i * nq + j, 0)),
        out_shape=jax.ShapeDtypeStruct((batch * seq, MLA_WIDTH), F32),
        scratch_shapes=[
            pltpu.VMEM((KV_LORA + ROPE_TILED, mq), BF16),
            pltpu.VMEM((1, mq), F32),
            pltpu.VMEM((1, mq), F32),
            pltpu.VMEM((KV_LORA, mq), F32),
        ],
        compiler_params=pltpu.CompilerParams(
            dimension_semantics=("arbitrary", "arbitrary"), vmem_limit_bytes=VMEM_LIMIT),
        name="mla_prompt",
    )(qlatt, qropet, kcat, ckvt3, kcat_meta, ckvt_meta, wuvt)


def _rev_cumsum_blocks(lk, tri):
    rows, width = lk.shape
    cb = tri.shape[0]
    nb = width // cb
    blocks = [lk[:, b * cb:(b + 1) * cb] for b in range(nb)]
    stacked = blocks[0] if nb == 1 else jnp.concatenate(blocks, axis=0)
    hi, lo = _split_bf16(stacked)
    loc = _dot(hi, tri) + _dot(lo, tri)
    tots = jnp.sum(stacked, axis=-1, keepdims=True)
    outs = [None] * nb
    off = jnp.zeros((rows, 1), F32)
    for b in range(nb - 1, -1, -1):
        outs[b] = loc[b * rows:(b + 1) * rows, :] + off
        off = off + tots[b * rows:(b + 1) * rows, :]
    later = outs[0] if nb == 1 else jnp.concatenate(outs, axis=1)
    return later, off


def _sb_weights(z, tri, mask, carry):
    sp = _softplus(z)
    lk = -sp if mask is None else jnp.where(mask, -sp, 0.0)
    later, total = _rev_cumsum_blocks(lk, tri)
    w = jnp.exp((z - sp) + (later + carry))
    if mask is not None:
        w = jnp.where(mask, w, 0.0)
    return w.astype(BF16), carry + total


def _softmax_update(s, mask, m, l):
    if mask is not None:
        s = jnp.where(mask, s, NEG_INF)
    m_new = jnp.maximum(m, jnp.max(s, axis=-1, keepdims=True))
    alpha = jnp.exp(m - m_new)
    p = jnp.exp(s - m_new)
    return p.astype(BF16), alpha, m_new, alpha * l + jnp.sum(p, axis=-1, keepdims=True)


def _head_diag_rows(full):
    rows, width = full.shape
    nq = rows // SB_HEADS
    row_head = lax.broadcasted_iota(jnp.int32, (rows, width), 0) // nq
    col_head = lax.broadcasted_iota(jnp.int32, (rows, width), 1) // SB_HEAD_DIM
    kept = jnp.where(row_head == col_head, full, 0.0)
    out = kept[0:nq, :]
    for hd in range(1, SB_HEADS):
        out = out + kept[hd * nq:(hd + 1) * nq, :]
    return out


def _decode_kernel(pt_ref, qbd_ref, qlat_ref, qrope_ref, knew_ref, vnew_ref, cnew_ref, rnew_ref,
                   tri_ref, wuv_ref, ck_hbm, cv_hbm, cc_hbm, cr_hbm, oa_ref, ob_ref,
                   kbuf, vbuf, cbuf, rbuf, acct_ref, sem):
    s = pl.program_id(0)
    n_seq = pl.num_programs(0)
    n_pages = pt_ref.shape[1]
    g = PAGES_PER_CHUNK
    n_chunks = n_pages // g
    page = cbuf.shape[1] // g

    def copies(seq, chunk, slot):
        out = []
        for p in range(g):
            pg = pt_ref[seq, chunk * g + p]
            span = pl.ds(p * page, page)
            out.append(pltpu.make_async_copy(ck_hbm.at[pg], kbuf.at[slot, :, span], sem.at[slot]))
            out.append(pltpu.make_async_copy(cv_hbm.at[pg], vbuf.at[slot, :, span], sem.at[slot]))
            out.append(pltpu.make_async_copy(cc_hbm.at[pg], cbuf.at[slot, span, :], sem.at[slot]))
            out.append(pltpu.make_async_copy(cr_hbm.at[pg], rbuf.at[slot, :, span], sem.at[slot]))
        return out

    def start_chunk(seq, chunk, slot):
        for c in copies(seq, chunk, slot):
            c.start()

    def wait_chunk(seq, chunk, slot):
        for c in copies(seq, chunk, slot):
            c.wait()

    def start_global(gidx):
        seq = lax.div(gidx, n_chunks)
        start_chunk(seq, n_chunks - 1 - lax.rem(gidx, n_chunks), lax.rem(gidx, N_SLOTS))

    @pl.when(s == 0)
    def _():
        for ahead in range(N_SLOTS - 1):
            start_chunk(ahead // n_chunks, n_chunks - 1 - ahead % n_chunks, ahead % N_SLOTS)

    qbd = qbd_ref[...]
    qlat = qlat_ref[...]
    qrope = qrope_ref[...]
    tri = tri_ref[...]
    rows = qbd.shape[0]
    nq = rows // SB_HEADS

    n_new = knew_ref.shape[0]

    def pad_new(ref):
        x = ref[...]
        return jnp.concatenate([x, jnp.zeros((page - n_new, x.shape[1]), x.dtype)], axis=0).astype(BF16)

    q_idx = lax.broadcasted_iota(jnp.int32, (rows, page), 0) & (nq - 1)
    k_idx = lax.broadcasted_iota(jnp.int32, (rows, page), 1)
    w, carry = _sb_weights(_dot_nt(qbd, pad_new(knew_ref)), tri_ref[0:page, 0:page], k_idx < q_idx,
                           jnp.zeros((rows, 1), F32))
    oa_new = _dot(w, pad_new(vnew_ref))
    cnew = pad_new(cnew_ref)
    p, _, m, l = _softmax_update(_dot_nt(qlat, cnew) + _dot_nt(qrope, pad_new(rnew_ref)),
                                 k_idx <= q_idx, jnp.full((rows, 1), NEG_INF, F32),
                                 jnp.zeros((rows, 1), F32))
    acc_b = _dot(p, cnew)
    acct_ref[...] = jnp.zeros(acct_ref.shape, F32)

    def body(k, state):
        carry, m, l, acc_b = state
        chunk = n_chunks - 1 - k
        gidx = s * n_chunks + k
        slot = lax.rem(gidx, N_SLOTS)

        @pl.when(gidx + (N_SLOTS - 1) < n_seq * n_chunks)
        def _():
            start_global(gidx + (N_SLOTS - 1))

        wait_chunk(s, chunk, slot)
        w, carry = _sb_weights(_dot(qbd, kbuf[slot].astype(BF16)), tri, None, carry)
        acct_ref[:, 0:rows] += _dot_nt(vbuf[slot].astype(BF16), w)
        cc = cbuf[slot].astype(BF16)
        p, alpha, m, l = _softmax_update(_dot_nt(qlat, cc) + _dot(qrope, rbuf[slot].astype(BF16)),
                                         None, m, l)
        acc_b = alpha * acc_b + _dot(p, cc)
        return carry, m, l, acc_b

    carry, m, l, acc_b = lax.fori_loop(0, n_chunks, body, (carry, m, l, acc_b))

    oa_ref[...] = _head_diag_rows(acct_ref[...].T[0:rows, :] + oa_new)
    o_lat = (acc_b / l).astype(BF16)
    ob_ref[...] = _head_diag_rows(_dot(o_lat, wuv_ref[...]))


def _run_decode(page_table, qbd, qlat, qrope, knew, vnew, cnew, rnew, tri, wuv_all,
                cache_kt, cache_vt, cache_c, cache_rt):
    n_seq, n_pages = page_table.shape
    page = cache_c.shape[1]
    rows = qbd.shape[1]
    nq = rows // SB_HEADS
    n_new = knew.shape[1]
    tok = PAGES_PER_CHUNK * page
    per_seq = lambda i, pt: (i, 0, 0)
    const2 = lambda i, pt: (0, 0)
    grid_spec = pltpu.PrefetchScalarGridSpec(
        num_scalar_prefetch=1,
        grid=(n_seq,),
        in_specs=[
            pl.BlockSpec((None, rows, SB_WIDTH), per_seq),
            pl.BlockSpec((None, rows, KV_LORA), per_seq),
            pl.BlockSpec((None, rows, MLA_ROPE), per_seq),
            pl.BlockSpec((None, n_new, SB_WIDTH), per_seq),
            pl.BlockSpec((None, n_new, SB_WIDTH), per_seq),
            pl.BlockSpec((None, n_new, KV_LORA), per_seq),
            pl.BlockSpec((None, n_new, MLA_ROPE), per_seq),
            pl.BlockSpec(tri.shape, const2),
            pl.BlockSpec(wuv_all.shape, const2),
            pl.BlockSpec(memory_space=pl.ANY),
            pl.BlockSpec(memory_space=pl.ANY),
            pl.BlockSpec(memory_space=pl.ANY),
            pl.BlockSpec(memory_space=pl.ANY),
        ],
        out_specs=[
            pl.BlockSpec((None, nq, SB_WIDTH), per_seq),
            pl.BlockSpec((None, nq, MLA_WIDTH), per_seq),
        ],
        scratch_shapes=[
            pltpu.VMEM((N_SLOTS, SB_WIDTH, tok), F32),
            pltpu.VMEM((N_SLOTS, SB_WIDTH, tok), F32),
            pltpu.VMEM((N_SLOTS, tok, KV_LORA), F32),
            pltpu.VMEM((N_SLOTS, MLA_ROPE, tok), F32),
            pltpu.VMEM((SB_WIDTH, LANES), F32),
            pltpu.SemaphoreType.DMA((N_SLOTS,)),
        ],
    )
    return pl.pallas_call(
        _decode_kernel,
        grid_spec=grid_spec,
        out_shape=[jax.ShapeDtypeStruct((n_seq, nq, SB_WIDTH), F32),
                   jax.ShapeDtypeStruct((n_seq, nq, MLA_WIDTH), F32)],
        compiler_params=pltpu.CompilerParams(
            dimension_semantics=("arbitrary",), vmem_limit_bytes=VMEM_LIMIT),
        name="decode",
    )(page_table, qbd, qlat, qrope, knew, vnew, cnew, rnew, tri, wuv_all,
      cache_kt, cache_vt, cache_c, cache_rt)


def _out_kernel(x_ref, oa_ref, ob_ref, sza_ref, szb_ref, ga_ref, gb_ref, wpa_ref, wpb_ref,
                wout_ref, gpost_ref, y_ref):
    a = (oa_ref[...] * sza_ref[...]).astype(BF16)
    b = (ob_ref[...] * szb_ref[...]).astype(BF16)
    merged = ga_ref[...] * _dot(a, wpa_ref[...]) + gb_ref[...] * _dot(b, wpb_ref[...])
    t = _dot(merged.astype(BF16), wout_ref[...])
    y_ref[...] = x_ref[...] + _rms(t, gpost_ref[...])


def _run_out(x2d, oa, ob, sza, szb, ga, gb, wpa, wpb, wout, gpost, tm):
    rows = x2d.shape[0]
    row = lambda t: (t, 0)
    const2 = lambda t: (0, 0)
    return pl.pallas_call(
        _out_kernel,
        grid=(rows // tm,),
        in_specs=[
            pl.BlockSpec((tm, D_MODEL), row),
            pl.BlockSpec((tm, SB_WIDTH), row),
            pl.BlockSpec((tm, MLA_WIDTH), row),
            pl.BlockSpec((tm, SB_WIDTH), row),
            pl.BlockSpec((tm, MLA_WIDTH), row),
            pl.BlockSpec((tm, D_MODEL), row),
            pl.BlockSpec((tm, D_MODEL), row),
            pl.BlockSpec(wpa.shape, const2),
            pl.BlockSpec(wpb.shape, const2),
            pl.BlockSpec(wout.shape, const2),
            pl.BlockSpec((1, D_MODEL), const2),
        ],
        out_specs=pl.BlockSpec((tm, D_MODEL), row),
        out_shape=jax.ShapeDtypeStruct((rows, D_MODEL), F32),
        compiler_params=pltpu.CompilerParams(
            dimension_semantics=("arbitrary",), vmem_limit_bytes=VMEM_LIMIT),
        name="out_mix",
    )(x2d, oa, ob, sza, szb, ga, gb, wpa, wpb, wout, gpost)


def _rope_tables(pos):
    half = MLA_ROPE // 2
    inv = ROPE_THETA ** (-jnp.arange(half, dtype=jnp.float32) * 2.0 / MLA_ROPE)
    ang = pos.astype(jnp.float32)[:, None] * inv[None, :]
    cos, sin = jnp.cos(ang), jnp.sin(ang)
    cos_t = jnp.repeat(cos, 2, axis=1)
    sin_t = jnp.stack([-sin, sin], axis=-1).reshape(pos.shape[0], MLA_ROPE)
    return jnp.tile(cos_t, (1, MLA_HEADS)), jnp.tile(sin_t, (1, MLA_HEADS))


def kernel(x_prompt, x_sample, cache_sb_k, cache_sb_v, cache_mla_ckv, cache_mla_krope, page_table,
           meta_tokens, g_pre, w_in, g_qnorm, w_uq, g_kvnorm, w_uk, w_uv, w_proj_a, w_proj_b,
           w_out, g_post):
    assert g_pre.shape[0] == 1, "single layer step"
    batch, seq, d_model = x_prompt.shape
    n_dec, dec_seq, _ = x_sample.shape
    n_pool, page = cache_sb_k.shape[1], cache_sb_k.shape[2]
    past_len = page_table.shape[1] * page
    pair_swap = jnp.arange(MLA_ROPE) ^ 1

    w = w_in[0]
    o_k, o_ckv = SB_WIDTH, 4 * SB_WIDTH + Q_LORA
    o_kr = o_ckv + KV_LORA
    w_kr = w[:, o_kr:o_kr + MLA_ROPE]
    w_aug = jnp.concatenate(
        [w[:, :o_kr], w[:, o_kr + MLA_ROPE:], jnp.tile(w_kr, (1, MLA_HEADS)),
         jnp.tile(w_kr[:, pair_swap], (1, MLA_HEADS))], axis=1).astype(BF16)
    assert w_aug.shape[1] == _N_AUG
    wkt = w[:, o_k:o_k + SB_WIDTH].T.astype(BF16)
    wct = w[:, o_ckv:o_ckv + KV_LORA].T.astype(BF16)
    wq = w_uq[0].reshape(Q_LORA, MLA_HEADS, MLA_NOPE + MLA_ROPE)
    wq_nope = jnp.pad(wq[:, :, :MLA_NOPE], ((0, 0), (0, 0), (0, NOPE_PAD - MLA_NOPE)))
    wq_rope = wq[:, :, MLA_NOPE:]
    wuqt = jnp.concatenate(
        [wq_nope.reshape(Q_LORA, -1), wq_rope.reshape(Q_LORA, -1),
         wq_rope[:, :, pair_swap].reshape(Q_LORA, -1)], axis=1).T.astype(BF16)
    assert wuqt.shape[0] == _N_UQ
    wuk_pad = jnp.pad(w_uk[0], ((0, 0), (0, 0), (0, NOPE_PAD - MLA_NOPE))).astype(BF16)
    wuv_all = jnp.swapaxes(w_uv[0], 0, 1).reshape(KV_LORA, MLA_WIDTH).astype(BF16)
    wuvt = wuv_all.T
    wpa = w_proj_a[0].astype(BF16)
    wpb = w_proj_b[0].astype(BF16)
    wout = w_out[0].astype(BF16)
    idx = jnp.arange(KEY_BLOCK)
    tri = (idx[:, None] > idx[None, :]).astype(BF16)
    weights = (g_pre, w_aug, wkt, wct, g_qnorm, wuqt, wuk_pad, g_kvnorm, g_kvnorm.reshape(KV_LORA, 1))

    xp2 = x_prompt.reshape(batch * seq, d_model)
    cos_p, sin_p = _rope_tables(N_META + jnp.arange(seq, dtype=jnp.int32))
    proj_p = _run_proj(xp2, cos_p, sin_p, seq // KEY_BLOCK, weights, KEY_BLOCK)
    n_s = n_dec * dec_seq
    n_small = -(-(n_s + N_META) // SMALL_TM) * SMALL_TM
    xs2 = x_sample.reshape(n_s, d_model)
    x_small = jnp.concatenate([xs2, meta_tokens.astype(x_prompt.dtype),
                               jnp.zeros((n_small - n_s - N_META, d_model), x_prompt.dtype)], axis=0)
    pos_small = jnp.concatenate([past_len + (jnp.arange(n_s, dtype=jnp.int32) % dec_seq),
                                 jnp.arange(n_small - n_s, dtype=jnp.int32)])
    cos_s, sin_s = _rope_tables(pos_small)
    proj_s = _run_proj(x_small, cos_s, sin_s, n_small // SMALL_TM, weights, SMALL_TM)

    (qsb_p, k_p, kt_p, v_p, vbf_p, sza_p, szb_p, ga_p, gb_p, qlatt_p, qropet_p, ckv_p, kcat_p,
     ckvt_p, krope_p) = proj_p
    (qsb_s, k_s, kt_s, v_s, vbf_s, sza_s, szb_s, ga_s, gb_s, qlatt_s, qropet_s, ckv_s, kcat_s,
     ckvt_s, krope_s) = proj_s
    assert n_s % SMALL_TM == 0
    meta_blk = n_s // SMALL_TM

    oa_p = _run_sb_prompt(qsb_p, kt_p, vbf_p, kt_s[meta_blk], vbf_s[n_s:n_s + SMALL_TM], tri,
                          batch, seq)
    ob_p = _run_mla_prompt(qlatt_p, qropet_p, kcat_p, ckvt_p, kcat_s[n_s:n_s + SMALL_TM],
                           ckvt_s[meta_blk], wuvt, batch, seq)
    y_p = _run_out(xp2, oa_p, ob_p, sza_p, szb_p, ga_p, gb_p, wpa, wpb, wout, g_post, OUT_TM)

    rows = SB_HEADS * dec_seq
    q4 = qsb_s[:n_s].reshape(n_dec, dec_seq, SB_WIDTH)
    qbd = jnp.tile(q4, (1, SB_HEADS, 1)).reshape(n_dec, SB_HEADS, dec_seq, SB_WIDTH)
    own = (jnp.arange(SB_WIDTH)[None, :] // SB_HEAD_DIM) == jnp.arange(SB_HEADS)[:, None]
    qbd = jnp.where(own[None, :, None, :], qbd, jnp.zeros_like(qbd)).reshape(n_dec, rows, SB_WIDTH)
    qlat_d = qlatt_s[:, :n_s].reshape(MLA_HEADS, KV_LORA, n_dec, dec_seq).transpose(2, 0, 3, 1)
    qlat_d = qlat_d.reshape(n_dec, rows, KV_LORA)
    qrope_d = qropet_s[:, :n_s].reshape(MLA_HEADS, MLA_ROPE, n_dec, dec_seq).transpose(2, 0, 3, 1)
    qrope_d = qrope_d.reshape(n_dec, rows, MLA_ROPE)

    def new_rows(a):
        a = a[:n_s].reshape(n_dec, dec_seq, a.shape[-1])
        return jnp.pad(a, ((0, 0), (0, 8 - dec_seq), (0, 0)))

    cache_kt = cache_sb_k[0].transpose(0, 2, 3, 1).reshape(n_pool, SB_WIDTH, page)
    cache_vt = cache_sb_v[0].transpose(0, 2, 3, 1).reshape(n_pool, SB_WIDTH, page)
    cache_rt = cache_mla_krope[0].transpose(0, 2, 1)
    oa_s, ob_s = _run_decode(
        page_table, qbd, qlat_d, qrope_d, new_rows(k_s), new_rows(v_s), new_rows(ckv_s),
        new_rows(krope_s), tri, wuv_all, cache_kt, cache_vt, cache_mla_ckv[0], cache_rt)
    y_s = _run_out(xs2, oa_s.reshape(n_s, SB_WIDTH), ob_s.reshape(n_s, MLA_WIDTH), sza_s[:n_s],
                   szb_s[:n_s], ga_s[:n_s], gb_s[:n_s], wpa, wpb, wout, g_post, n_s)

    def with_meta(small, big):
        width = big.shape[-1]
        meta = jnp.broadcast_to(small[n_s:n_s + N_META][None], (batch, N_META, width))
        return jnp.concatenate([meta, big.reshape(batch, seq, width)], axis=1)[None]

    lp = seq + N_META
    return (
        y_p.reshape(batch, seq, d_model),
        y_s.reshape(n_dec, dec_seq, d_model),
        with_meta(k_s, k_p).reshape(1, batch, lp, SB_HEADS, SB_HEAD_DIM),
        with_meta(v_s, v_p).reshape(1, batch, lp, SB_HEADS, SB_HEAD_DIM),
        with_meta(ckv_s, ckv_p),
        with_meta(krope_s, krope_p),
        k_s[:n_s].reshape(1, n_dec, dec_seq, SB_HEADS, SB_HEAD_DIM),
        v_s[:n_s].reshape(1, n_dec, dec_seq, SB_HEADS, SB_HEAD_DIM),
        ckv_s[:n_s].reshape(1, n_dec, dec_seq, KV_LORA),
        krope_s[:n_s].reshape(1, n_dec, dec_seq, MLA_ROPE),
    )
```

```python
import jax
import jax.numpy as jnp
from jax import lax
from jax.experimental import pallas as pl
from jax.experimental.pallas import tpu as pltpu

F32 = jnp.float32
BF16 = jnp.bfloat16

N_META = 16
SB_HEADS = 8
SB_HEAD_DIM = 64
SB_WIDTH = SB_HEADS * SB_HEAD_DIM
MLA_HEADS = 8
MLA_NOPE = 64
MLA_ROPE = 32
MLA_V = 64
MLA_WIDTH = MLA_HEADS * MLA_V
Q_LORA = 384
KV_LORA = 256
D_MODEL = 1024
MLA_SCALE = (MLA_NOPE + MLA_ROPE) ** -0.5
SB_SCALE = SB_HEAD_DIM ** -0.5
ROPE_THETA = 10000.0
RMS_EPS = 1e-6
NEG_INF = -1e30

LANES = 128
ROPE_TILED = MLA_HEADS * MLA_ROPE
NOPE_PAD = LANES
PAIR = 2 * SB_HEAD_DIM
N_PAIRS = SB_WIDTH // PAIR

_C_Q, _C_K, _C_V, _C_ZA = 0, 512, 1024, 1536
_C_CQ = 2048
_C_CKV = _C_CQ + Q_LORA
_C_ZB = _C_CKV + KV_LORA
_C_GA = _C_ZB + MLA_WIDTH
_C_GB = _C_GA + D_MODEL
_C_KR = _C_GB + D_MODEL
_C_KRS = _C_KR + ROPE_TILED
_N_AUG = _C_KRS + ROPE_TILED
_R_ROPE = MLA_HEADS * NOPE_PAD
_R_ROPES = _R_ROPE + ROPE_TILED
_N_UQ = _R_ROPES + ROPE_TILED

VMEM_LIMIT = 56 * 1024 * 1024

KEY_BLOCK = 256
SMALL_TM = 128
OUT_TM = 512
MLA_TQ = 256
PAGES_PER_CHUNK = 16
N_SLOTS = 3


def _dot(a, b):
    return jnp.dot(a, b, preferred_element_type=F32)


def _dot_nt(a, b):
    return lax.dot_general(a, b, (((1,), (1,)), ((), ())), preferred_element_type=F32)


def _rms(x, g):
    ms = jnp.mean(x * x, axis=-1, keepdims=True)
    return x * lax.rsqrt(ms + RMS_EPS) * g


def _softplus(z):
    return jnp.maximum(z, 0.0) + jnp.log(1.0 + jnp.exp(-jnp.abs(z)))


def _sigmoid(z):
    return 1.0 / (1.0 + jnp.exp(-z))


def _split_bf16(x):
    hi = x.astype(BF16)
    lo = (x - hi.astype(F32)).astype(BF16)
    return hi, lo


def _proj_kernel(x_ref, c_ref, s_ref, ct_ref, st_ref, gpre_ref, w_ref, wqt_ref, wvt_ref, wct_ref, gq_ref,
                 wuqt_ref, wuk_ref, gkv_ref, gkvt_ref,
                 qt_ref, k_ref, kbf_ref, v_ref, vt_ref, sza_ref, szb_ref, ga_ref, gb_ref,
                 qlatt_ref, qropet_ref, ckv_ref, kcat_ref, ckvt_ref, krope_ref):
    h = _rms(x_ref[...], gpre_ref[...]).astype(BF16)

    def proj(a, b):
        return _dot(h, w_ref[:, a:b])

    qt_ref[...] = (_dot_nt(wqt_ref[...], h) * SB_SCALE).astype(BF16)
    r = proj(_C_K, _C_V)
    k_ref[...] = r
    kbf_ref[...] = r.astype(BF16)
    v_ref[...] = proj(_C_V, _C_ZA)
    vt_ref[...] = _dot_nt(wvt_ref[...], h).astype(BF16)
    r = proj(_C_ZA, _C_CQ)
    sza_ref[...] = r * _sigmoid(r)
    r = proj(_C_ZB, _C_GA)
    szb_ref[...] = r * _sigmoid(r)
    ga_ref[...] = _sigmoid(proj(_C_GA, _C_GB))
    gb_ref[...] = _sigmoid(proj(_C_GB, _C_KR))

    ckv = _rms(proj(_C_CKV, _C_ZB), gkv_ref[...])
    ckv_ref[...] = ckv
    kcat_ref[:, 0:KV_LORA] = ckv.astype(BF16)
    ct = _dot_nt(wct_ref[...], h)
    ms = jnp.mean(ct * ct, axis=0, keepdims=True)
    ckvt_ref[...] = (ct * lax.rsqrt(ms + RMS_EPS) * gkvt_ref[...]).astype(BF16)
    kr = proj(_C_KR, _C_KRS) * c_ref[...] + proj(_C_KRS, _N_AUG) * s_ref[...]
    krope_ref[...] = kr[:, 0:MLA_ROPE]
    kcat_ref[:, KV_LORA:KV_LORA + ROPE_TILED] = kr.astype(BF16)

    cq = _rms(proj(_C_CQ, _C_CKV), gq_ref[...]).astype(BF16)
    qmt = _dot_nt(wuqt_ref[...], cq)
    for hd in range(MLA_HEADS):
        qn = qmt[hd * NOPE_PAD:(hd + 1) * NOPE_PAD, :].astype(BF16)
        ql = _dot(wuk_ref[hd], qn) * MLA_SCALE
        qlatt_ref[hd * KV_LORA:(hd + 1) * KV_LORA, :] = ql.astype(BF16)
    qr = qmt[_R_ROPE:_R_ROPES, :] * ct_ref[...] + qmt[_R_ROPES:_N_UQ, :] * st_ref[...]
    qropet_ref[...] = (qr * MLA_SCALE).astype(BF16)


def _run_proj(x2d, cos_t, sin_t, n_table_blocks, weights, tm):
    gpre, w_aug, wqt, wvt, wct, gq, wuqt, wuk_pad, gkv, gkvt = weights
    rows = x2d.shape[0]
    nblk = rows // tm
    row = lambda t: (t, 0)
    col = lambda t: (0, t)
    tab = lambda t: (t % n_table_blocks, 0)
    tabt = lambda t: (0, t % n_table_blocks)
    const2 = lambda t: (0, 0)
    const3 = lambda t: (0, 0, 0)

    def out(width, dtype):
        return jax.ShapeDtypeStruct((rows, width), dtype), pl.BlockSpec((tm, width), row)

    def out_t(height, dtype):
        return jax.ShapeDtypeStruct((height, rows), dtype), pl.BlockSpec((height, tm), col)

    def out_blocked(height, dtype):
        return (jax.ShapeDtypeStruct((nblk, height, tm), dtype),
                pl.BlockSpec((None, height, tm), lambda t: (t, 0, 0)))

    outs = [out_t(SB_WIDTH, BF16), out(SB_WIDTH, F32), out(SB_WIDTH, BF16), out(SB_WIDTH, F32),
            out_blocked(SB_WIDTH, BF16), out(SB_WIDTH, F32), out(MLA_WIDTH, F32), out(D_MODEL, F32),
            out(D_MODEL, F32), out_t(MLA_HEADS * KV_LORA, BF16), out_t(ROPE_TILED, BF16),
            out(KV_LORA, F32), out(KV_LORA + ROPE_TILED, BF16), out_blocked(KV_LORA, BF16),
            out(MLA_ROPE, F32)]
    cos_tt, sin_tt = cos_t.T, sin_t.T
    return pl.pallas_call(
        _proj_kernel,
        grid=(nblk,),
        in_specs=[
            pl.BlockSpec((tm, D_MODEL), row),
            pl.BlockSpec((tm, ROPE_TILED), tab),
            pl.BlockSpec((tm, ROPE_TILED), tab),
            pl.BlockSpec((ROPE_TILED, tm), tabt),
            pl.BlockSpec((ROPE_TILED, tm), tabt),
            pl.BlockSpec((1, D_MODEL), const2),
            pl.BlockSpec(w_aug.shape, const2),
            pl.BlockSpec(wqt.shape, const2),
            pl.BlockSpec(wvt.shape, const2),
            pl.BlockSpec(wct.shape, const2),
            pl.BlockSpec((1, Q_LORA), const2),
            pl.BlockSpec(wuqt.shape, const2),
            pl.BlockSpec(wuk_pad.shape, const3),
            pl.BlockSpec((1, KV_LORA), const2),
            pl.BlockSpec((KV_LORA, 1), const2),
        ],
        out_specs=[o[1] for o in outs],
        out_shape=[o[0] for o in outs],
        compiler_params=pltpu.CompilerParams(
            dimension_semantics=("arbitrary",), vmem_limit_bytes=VMEM_LIMIT),
        name="proj",
    )(x2d, cos_t, sin_t, cos_tt, sin_tt, gpre, w_aug, wqt, wvt, wct, gq, wuqt, wuk_pad, gkv, gkvt)


def _sb_prompt_kernel(qt_ref, k_ref, vt_ref, km_ref, vtm_ref, trit_ref, o_ref, q2_ref, carry_ref, acc_ref):
    qi = pl.program_id(1)
    t = KEY_BLOCK
    trit = trit_ref[...]
    trit2 = jnp.concatenate([trit, trit], axis=1)
    top = lax.broadcasted_iota(jnp.int32, (PAIR, t), 0) < SB_HEAD_DIM
    pairs = range(N_PAIRS)

    for p in pairs:
        qp = qt_ref[p * PAIR:(p + 1) * PAIR, :]
        zero = jnp.zeros_like(qp)
        q2_ref[p, :, 0:t] = jnp.where(top, qp, zero)
        q2_ref[p, :, t:2 * t] = jnp.where(top, zero, qp)
    carry_ref[...] = jnp.zeros(carry_ref.shape, F32)
    acc_ref[...] = jnp.zeros(acc_ref.shape, F32)

    def sweep(kb_of, vtb_of, tr2, mask):
        zs = [_dot(kb_of(p), q2_ref[p]) for p in pairs]
        sps = [_softplus(z) for z in zs]
        kept = sps if mask is None else [jnp.where(mask, sp, 0.0) for sp in sps]
        splits = [_split_bf16(sp) for sp in kept]
        locs = [_dot(tr2, jnp.concatenate([hi, lo], axis=0)) for hi, lo in splits]
        ws = [jnp.exp((z - sp) - (loc + carry_ref[p])) for p, z, sp, loc in zip(pairs, zs, sps, locs)]
        if mask is not None:
            ws = [jnp.where(mask, w, 0.0) for w in ws]
        for p in pairs:
            acc_ref[p] += _dot(vtb_of(p), ws[p].astype(BF16))
            carry_ref[p] += locs[p][0:1, :] + kept[p][0:1, :]

    def own_block(kb, mask):
        start = pl.multiple_of(kb * t, t)
        sweep(lambda p: k_ref[pl.ds(start, t), p * PAIR:(p + 1) * PAIR],
              lambda p: vt_ref[kb, p * PAIR:(p + 1) * PAIR, :], trit2, mask)

    k_in = lax.broadcasted_iota(jnp.int32, (t, 2 * t), 0)
    q_in = lax.broadcasted_iota(jnp.int32, (t, 2 * t), 1) & (t - 1)
    own_block(qi, k_in < q_in)

    def body(j, _):
        own_block(qi - 1 - j, None)
        return 0

    lax.fori_loop(0, qi, body, 0)

    tm = km_ref.shape[0]
    m_in = lax.broadcasted_iota(jnp.int32, (tm, 2 * t), 0)
    tritm = trit_ref[0:tm, 0:tm]
    sweep(lambda p: km_ref[:, p * PAIR:(p + 1) * PAIR], lambda p: vtm_ref[p * PAIR:(p + 1) * PAIR, :],
          jnp.concatenate([tritm, tritm], axis=1), m_in < N_META)

    for p in pairs:
        acc = acc_ref[p]
        o_ref[:, p * PAIR:(p + 1) * PAIR] = jnp.where(top, acc[:, 0:t], acc[:, t:2 * t]).T


def _run_sb_prompt(qt, kbf, vt3, k_meta, vt_meta, trit, batch, seq):
    t = KEY_BLOCK
    nq = seq // t
    return pl.pallas_call(
        _sb_prompt_kernel,
        grid=(batch, nq),
        in_specs=[
            pl.BlockSpec((SB_WIDTH, t), lambda i, j: (0, i * nq + j)),
            pl.BlockSpec((seq, SB_WIDTH), lambda i, j: (i, 0)),
            pl.BlockSpec((nq, SB_WIDTH, t), lambda i, j: (i, 0, 0)),
            pl.BlockSpec(k_meta.shape, lambda i, j: (0, 0)),
            pl.BlockSpec(vt_meta.shape, lambda i, j: (0, 0)),
            pl.BlockSpec(trit.shape, lambda i, j: (0, 0)),
        ],
        out_specs=pl.BlockSpec((t, SB_WIDTH), lambda i, j: (i * nq + j, 0)),
        out_shape=jax.ShapeDtypeStruct((batch * seq, SB_WIDTH), F32),
        scratch_shapes=[
            pltpu.VMEM((N_PAIRS, PAIR, 2 * t), BF16),
            pltpu.VMEM((N_PAIRS, 1, 2 * t), F32),
            pltpu.VMEM((N_PAIRS, PAIR, 2 * t), F32),
        ],
        compiler_params=pltpu.CompilerParams(
            dimension_semantics=("arbitrary", "arbitrary"), vmem_limit_bytes=VMEM_LIMIT),
        name="sb_prompt",
    )(qt, kbf, vt3, k_meta, vt_meta, trit)


def _mla_prompt_kernel(qlatt_ref, qropet_ref, kcat_ref, ckvt_ref, kmeta_ref, ctmeta_ref, wuvt_ref,
                       o_ref, qcat_ref, m_ref, l_ref, acc_ref):
    qi = pl.program_id(1)
    tq, tk = MLA_TQ, KEY_BLOCK
    mq = MLA_HEADS * tq

    sub = lax.broadcasted_iota(jnp.int32, (ROPE_TILED, tq), 0)
    qrt = qropet_ref[...]
    for hd in range(MLA_HEADS):
        qcat_ref[0:KV_LORA, hd * tq:(hd + 1) * tq] = qlatt_ref[hd * KV_LORA:(hd + 1) * KV_LORA, :]
        own = (sub >= hd * MLA_ROPE) & (sub < (hd + 1) * MLA_ROPE)
        qcat_ref[KV_LORA:, hd * tq:(hd + 1) * tq] = jnp.where(own, qrt, jnp.zeros_like(qrt))
    qcat = qcat_ref[...]

    m_ref[...] = jnp.full(m_ref.shape, NEG_INF, F32)
    l_ref[...] = jnp.zeros(l_ref.shape, F32)
    acc_ref[...] = jnp.zeros(acc_ref.shape, F32)

    def step(kc, ct, mask):
        s = _dot(kc, qcat)
        if mask is not None:
            s = jnp.where(mask, s, NEG_INF)
        m_old = m_ref[...]
        m_new = jnp.maximum(m_old, jnp.max(s, axis=0, keepdims=True))
        alpha = jnp.exp(m_old - m_new)
        p = jnp.exp(s - m_new)
        l_ref[...] = alpha * l_ref[...] + jnp.sum(p, axis=0, keepdims=True)
        acc_ref[...] = alpha * acc_ref[...] + _dot(ct, p.astype(BF16))
        m_ref[...] = m_new

    jd = (qi * tq) // tk
    r_in = lax.broadcasted_iota(jnp.int32, (tk, mq), 0)
    c_in = lax.broadcasted_iota(jnp.int32, (tk, mq), 1) & (tq - 1)
    off = qi * tq - jd * tk
    step(kcat_ref[pl.ds(pl.multiple_of(jd * tk, tk), tk), :], ckvt_ref[jd], (r_in - c_in) <= off)

    def body(j, _):
        step(kcat_ref[pl.ds(pl.multiple_of(j * tk, tk), tk), :], ckvt_ref[j], None)
        return 0

    lax.fori_loop(0, jd, body, 0)

    tmeta = kmeta_ref.shape[0]
    mrow = lax.broadcasted_iota(jnp.int32, (tmeta, mq), 0)
    step(kmeta_ref[...], ctmeta_ref[...], mrow < N_META)

    o_lat = (acc_ref[...] / l_ref[...]).astype(BF16)
    full = _dot(wuvt_ref[...], o_lat)
    row_head = lax.broadcasted_iota(jnp.int32, (MLA_WIDTH, tq), 0) // MLA_V
    out_t = jnp.zeros((MLA_WIDTH, tq), F32)
    for hd in range(MLA_HEADS):
        out_t = out_t + jnp.where(row_head == hd, full[:, hd * tq:(hd + 1) * tq], 0.0)
    o_ref[...] = out_t.T


def _run_mla_prompt(qlatt, qropet, kcat, ckvt3, kcat_meta, ckvt_meta, wuvt, batch, seq):
    tq, tk = MLA_TQ, KEY_BLOCK
    nq = seq // tq
    nk = seq // tk
    mq = MLA_HEADS * tq
    return pl.pallas_call(
        _mla_prompt_kernel,
        grid=(batch, nq),
        in_specs=[
            pl.BlockSpec((qlatt.shape[0], tq), lambda i, j: (0, i * nq + j)),
            pl.BlockSpec((qropet.shape[0], tq), lambda i, j: (0, i * nq + j)),
            pl.BlockSpec((seq, kcat.shape[1]), lambda i, j: (i, 0)),
            pl.BlockSpec((nk, KV_LORA, tk), lambda i, j: (i, 0, 0)),
            pl.BlockSpec(kcat_meta.shape, lambda i, j: (0, 0)),
            pl.BlockSpec(ckvt_meta.shape, lambda i, j: (0, 0)),
            pl.BlockSpec(wuvt.shape, lambda i, j: (0, 0)),
        ],
        out_specs=pl.BlockSpec((tq, MLA_WIDTH), lambda i, j: (i * nq + j, 0)),
        out_shape=jax.ShapeDtypeStruct((batch * seq, MLA_WIDTH), F32),
        scratch_shapes=[
            pltpu.VMEM((KV_LORA + ROPE_TILED, mq), BF16),
            pltpu.VMEM((1, mq), F32),
            pltpu.VMEM((1, mq), F32),
            pltpu.VMEM((KV_LORA, mq), F32),
        ],
        compiler_params=pltpu.CompilerParams(
            dimension_semantics=("arbitrary", "arbitrary"), vmem_limit_bytes=VMEM_LIMIT),
        name="mla_prompt",
    )(qlatt, qropet, kcat, ckvt3, kcat_meta, ckvt_meta, wuvt)


def _rev_cumsum_blocks(lk, tri):
    rows, width = lk.shape
    cb = tri.shape[0]
    nb = width // cb
    blocks = [lk[:, b * cb:(b + 1) * cb] for b in range(nb)]
    stacked = blocks[0] if nb == 1 else jnp.concatenate(blocks, axis=0)
    hi, lo = _split_bf16(stacked)
    loc = _dot(hi, tri) + _dot(lo, tri)
    tots = jnp.sum(stacked, axis=-1, keepdims=True)
    outs = [None] * nb
    off = jnp.zeros((rows, 1), F32)
    for b in range(nb - 1, -1, -1):
        outs[b] = loc[b * rows:(b + 1) * rows, :] + off
        off = off + tots[b * rows:(b + 1) * rows, :]
    later = outs[0] if nb == 1 else jnp.concatenate(outs, axis=1)
    return later, off


def _sb_weights(z, tri, mask, carry):
    sp = _softplus(z)
    lk = -sp if mask is None else jnp.where(mask, -sp, 0.0)
    later, total = _rev_cumsum_blocks(lk, tri)
    w = jnp.exp((z - sp) + (later + carry))
    if mask is not None:
        w = jnp.where(mask, w, 0.0)
    return w.astype(BF16), carry + total


def _softmax_update(s, mask, m, l):
    if mask is not None:
        s = jnp.where(mask, s, NEG_INF)
    m_new = jnp.maximum(m, jnp.max(s, axis=-1, keepdims=True))
    alpha = jnp.exp(m - m_new)
    p = jnp.exp(s - m_new)
    return p.astype(BF16), alpha, m_new, alpha * l + jnp.sum(p, axis=-1, keepdims=True)


def _head_diag_rows(full):
    rows, width = full.shape
    nq = rows // SB_HEADS
    row_head = lax.broadcasted_iota(jnp.int32, (rows, width), 0) // nq
    col_head = lax.broadcasted_iota(jnp.int32, (rows, width), 1) // SB_HEAD_DIM
    kept = jnp.where(row_head == col_head, full, 0.0)
    out = kept[0:nq, :]
    for hd in range(1, SB_HEADS):
        out = out + kept[hd * nq:(hd + 1) * nq, :]
    return out


def _decode_kernel(pt_ref, qbd_ref, qlat_ref, qrope_ref, knew_ref, vnew_ref, cnew_ref, rnew_ref,
                   tri_ref, wuv_ref, ck_hbm, cv_hbm, cc_hbm, cr_hbm, oa_ref, ob_ref,
                   kbuf, vbuf, cbuf, rbuf, acct_ref, sem):
    s = pl.program_id(0)
    n_seq = pl.num_programs(0)
    n_pages = pt_ref.shape[1]
    g = PAGES_PER_CHUNK
    n_chunks = n_pages // g
    page = cbuf.shape[1] // g

    def copies(seq, chunk, slot):
        out = []
        for p in range(g):
            pg = pt_ref[seq, chunk * g + p]
            span = pl.ds(p * page, page)
            out.append(pltpu.make_async_copy(ck_hbm.at[pg], kbuf.at[slot, :, span], sem.at[slot]))
            out.append(pltpu.make_async_copy(cv_hbm.at[pg], vbuf.at[slot, :, span], sem.at[slot]))
            out.append(pltpu.make_async_copy(cc_hbm.at[pg], cbuf.at[slot, span, :], sem.at[slot]))
            out.append(pltpu.make_async_copy(cr_hbm.at[pg], rbuf.at[slot, :, span], sem.at[slot]))
        return out

    def start_chunk(seq, chunk, slot):
        for c in copies(seq, chunk, slot):
            c.start()

    def wait_chunk(seq, chunk, slot):
        for c in copies(seq, chunk, slot):
            c.wait()

    def start_global(gidx):
        seq = lax.div(gidx, n_chunks)
        start_chunk(seq, n_chunks - 1 - lax.rem(gidx, n_chunks), lax.rem(gidx, N_SLOTS))

    @pl.when(s == 0)
    def _():
        for ahead in range(N_SLOTS - 1):
            start_chunk(ahead // n_chunks, n_chunks - 1 - ahead % n_chunks, ahead % N_SLOTS)

    qbd = qbd_ref[...]
    qlat = qlat_ref[...]
    qrope = qrope_ref[...]
    tri = tri_ref[...]
    rows = qbd.shape[0]
    nq = rows // SB_HEADS

    n_new = knew_ref.shape[0]

    def pad_new(ref):
        x = ref[...]
        return jnp.concatenate([x, jnp.zeros((page - n_new, x.shape[1]), x.dtype)], axis=0).astype(BF16)

    q_idx = lax.broadcasted_iota(jnp.int32, (rows, page), 0) & (nq - 1)
    k_idx = lax.broadcasted_iota(jnp.int32, (rows, page), 1)
    w, carry = _sb_weights(_dot_nt(qbd, pad_new(knew_ref)), tri_ref[0:page, 0:page], k_idx < q_idx,
                           jnp.zeros((rows, 1), F32))
    oa_new = _dot(w, pad_new(vnew_ref))
    cnew = pad_new(cnew_ref)
    p, _, m, l = _softmax_update(_dot_nt(qlat, cnew) + _dot_nt(qrope, pad_new(rnew_ref)),
                                 k_idx <= q_idx, jnp.full((rows, 1), NEG_INF, F32),
                                 jnp.zeros((rows, 1), F32))
    acc_b = _dot(p, cnew)
    acct_ref[...] = jnp.zeros(acct_ref.shape, F32)

    def body(k, state):
        carry, m, l, acc_b = state
        chunk = n_chunks - 1 - k
        gidx = s * n_chunks + k
        slot = lax.rem(gidx, N_SLOTS)

        @pl.when(gidx + (N_SLOTS - 1) < n_seq * n_chunks)
        def _():
            start_global(gidx + (N_SLOTS - 1))

        wait_chunk(s, chunk, slot)
        w, carry = _sb_weights(_dot(qbd, kbuf[slot].astype(BF16)), tri, None, carry)
        acct_ref[:, 0:rows] += _dot_nt(vbuf[slot].astype(BF16), w)
        cc = cbuf[slot].astype(BF16)
        p, alpha, m, l = _softmax_update(_dot_nt(qlat, cc) + _dot(qrope, rbuf[slot].astype(BF16)),
                                         None, m, l)
        acc_b = alpha * acc_b + _dot(p, cc)
        return carry, m, l, acc_b

    carry, m, l, acc_b = lax.fori_loop(0, n_chunks, body, (carry, m, l, acc_b))

    oa_ref[...] = _head_diag_rows(acct_ref[...].T[0:rows, :] + oa_new)
    o_lat = (acc_b / l).astype(BF16)
    ob_ref[...] = _head_diag_rows(_dot(o_lat, wuv_ref[...]))


def _run_decode(page_table, qbd, qlat, qrope, knew, vnew, cnew, rnew, tri, wuv_all,
                cache_kt, cache_vt, cache_c, cache_rt):
    n_seq, n_pages = page_table.shape
    page = cache_c.shape[1]
    rows = qbd.shape[1]
    nq = rows // SB_HEADS
    n_new = knew.shape[1]
    tok = PAGES_PER_CHUNK * page
    per_seq = lambda i, pt: (i, 0, 0)
    const2 = lambda i, pt: (0, 0)
    grid_spec = pltpu.PrefetchScalarGridSpec(
        num_scalar_prefetch=1,
        grid=(n_seq,),
        in_specs=[
            pl.BlockSpec((None, rows, SB_WIDTH), per_seq),
            pl.BlockSpec((None, rows, KV_LORA), per_seq),
            pl.BlockSpec((None, rows, MLA_ROPE), per_seq),
            pl.BlockSpec((None, n_new, SB_WIDTH), per_seq),
            pl.BlockSpec((None, n_new, SB_WIDTH), per_seq),
            pl.BlockSpec((None, n_new, KV_LORA), per_seq),
            pl.BlockSpec((None, n_new, MLA_ROPE), per_seq),
            pl.BlockSpec(tri.shape, const2),
            pl.BlockSpec(wuv_all.shape, const2),
            pl.BlockSpec(memory_space=pl.ANY),
            pl.BlockSpec(memory_space=pl.ANY),
            pl.BlockSpec(memory_space=pl.ANY),
            pl.BlockSpec(memory_space=pl.ANY),
        ],
        out_specs=[
            pl.BlockSpec((None, nq, SB_WIDTH), per_seq),
            pl.BlockSpec((None, nq, MLA_WIDTH), per_seq),
        ],
        scratch_shapes=[
            pltpu.VMEM((N_SLOTS, SB_WIDTH, tok), F32),
            pltpu.VMEM((N_SLOTS, SB_WIDTH, tok), F32),
            pltpu.VMEM((N_SLOTS, tok, KV_LORA), F32),
            pltpu.VMEM((N_SLOTS, MLA_ROPE, tok), F32),
            pltpu.VMEM((SB_WIDTH, LANES), F32),
            pltpu.SemaphoreType.DMA((N_SLOTS,)),
        ],
    )
    return pl.pallas_call(
        _decode_kernel,
        grid_spec=grid_spec,
        out_shape=[jax.ShapeDtypeStruct((n_seq, nq, SB_WIDTH), F32),
                   jax.ShapeDtypeStruct((n_seq, nq, MLA_WIDTH), F32)],
        compiler_params=pltpu.CompilerParams(
            dimension_semantics=("arbitrary",), vmem_limit_bytes=VMEM_LIMIT),
        name="decode",
    )(page_table, qbd, qlat, qrope, knew, vnew, cnew, rnew, tri, wuv_all,
      cache_kt, cache_vt, cache_c, cache_rt)


def _out_kernel(x_ref, oa_ref, ob_ref, sza_ref, szb_ref, ga_ref, gb_ref, wpa_ref, wpb_ref,
                wout_ref, gpost_ref, y_ref):
    a = (oa_ref[...] * sza_ref[...]).astype(BF16)
    b = (ob_ref[...] * szb_ref[...]).astype(BF16)
    merged = ga_ref[...] * _dot(a, wpa_ref[...]) + gb_ref[...] * _dot(b, wpb_ref[...])
    t = _dot(merged.astype(BF16), wout_ref[...])
    y_ref[...] = x_ref[...] + _rms(t, gpost_ref[...])


def _run_out(x2d, oa, ob, sza, szb, ga, gb, wpa, wpb, wout, gpost, tm):
    rows = x2d.shape[0]
    row = lambda t: (t, 0)
    const2 = lambda t: (0, 0)
    return pl.pallas_call(
        _out_kernel,
        grid=(rows // tm,),
        in_specs=[
            pl.BlockSpec((tm, D_MODEL), row),
            pl.BlockSpec((tm, SB_WIDTH), row),
            pl.BlockSpec((tm, MLA_WIDTH), row),
            pl.BlockSpec((tm, SB_WIDTH), row),
            pl.BlockSpec((tm, MLA_WIDTH), row),
            pl.BlockSpec((tm, D_MODEL), row),
            pl.BlockSpec((tm, D_MODEL), row),
            pl.BlockSpec(wpa.shape, const2),
            pl.BlockSpec(wpb.shape, const2),
            pl.BlockSpec(wout.shape, const2),
            pl.BlockSpec((1, D_MODEL), const2),
        ],
        out_specs=pl.BlockSpec((tm, D_MODEL), row),
        out_shape=jax.ShapeDtypeStruct((rows, D_MODEL), F32),
        compiler_params=pltpu.CompilerParams(
            dimension_semantics=("arbitrary",), vmem_limit_bytes=VMEM_LIMIT),
        name="out_mix",
    )(x2d, oa, ob, sza, szb, ga, gb, wpa, wpb, wout, gpost)


def _rope_tables(pos):
    half = MLA_ROPE // 2
    inv = ROPE_THETA ** (-jnp.arange(half, dtype=jnp.float32) * 2.0 / MLA_ROPE)
    ang = pos.astype(jnp.float32)[:, None] * inv[None, :]
    cos, sin = jnp.cos(ang), jnp.sin(ang)
    cos_t = jnp.repeat(cos, 2, axis=1)
    sin_t = jnp.stack([-sin, sin], axis=-1).reshape(pos.shape[0], MLA_ROPE)
    return jnp.tile(cos_t, (1, MLA_HEADS)), jnp.tile(sin_t, (1, MLA_HEADS))


def kernel(x_prompt, x_sample, cache_sb_k, cache_sb_v, cache_mla_ckv, cache_mla_krope, page_table,
           meta_tokens, g_pre, w_in, g_qnorm, w_uq, g_kvnorm, w_uk, w_uv, w_proj_a, w_proj_b,
           w_out, g_post):
    assert g_pre.shape[0] == 1, "single layer step"
    batch, seq, d_model = x_prompt.shape
    n_dec, dec_seq, _ = x_sample.shape
    n_pool, page = cache_sb_k.shape[1], cache_sb_k.shape[2]
    past_len = page_table.shape[1] * page
    pair_swap = jnp.arange(MLA_ROPE) ^ 1

    w = w_in[0]
    o_ckv = 4 * SB_WIDTH + Q_LORA
    o_kr = o_ckv + KV_LORA
    w_kr = w[:, o_kr:o_kr + MLA_ROPE]
    w_aug = jnp.concatenate(
        [w[:, :o_kr], w[:, o_kr + MLA_ROPE:], jnp.tile(w_kr, (1, MLA_HEADS)),
         jnp.tile(w_kr[:, pair_swap], (1, MLA_HEADS))], axis=1).astype(BF16)
    assert w_aug.shape[1] == _N_AUG
    wqt = w[:, :SB_WIDTH].T.astype(BF16)
    wvt = w[:, 2 * SB_WIDTH:3 * SB_WIDTH].T.astype(BF16)
    wct = w[:, o_ckv:o_ckv + KV_LORA].T.astype(BF16)
    wq = w_uq[0].reshape(Q_LORA, MLA_HEADS, MLA_NOPE + MLA_ROPE)
    wq_nope = jnp.pad(wq[:, :, :MLA_NOPE], ((0, 0), (0, 0), (0, NOPE_PAD - MLA_NOPE)))
    wq_rope = wq[:, :, MLA_NOPE:]
    wuqt = jnp.concatenate(
        [wq_nope.reshape(Q_LORA, -1), wq_rope.reshape(Q_LORA, -1),
         wq_rope[:, :, pair_swap].reshape(Q_LORA, -1)], axis=1).T.astype(BF16)
    assert wuqt.shape[0] == _N_UQ
    wuk_pad = jnp.pad(w_uk[0], ((0, 0), (0, 0), (0, NOPE_PAD - MLA_NOPE))).astype(BF16)
    wuv_all = jnp.swapaxes(w_uv[0], 0, 1).reshape(KV_LORA, MLA_WIDTH).astype(BF16)
    wuvt = wuv_all.T
    wpa = w_proj_a[0].astype(BF16)
    wpb = w_proj_b[0].astype(BF16)
    wout = w_out[0].astype(BF16)
    idx = jnp.arange(KEY_BLOCK)
    tri = (idx[:, None] > idx[None, :]).astype(BF16)
    weights = (g_pre, w_aug, wqt, wvt, wct, g_qnorm, wuqt, wuk_pad, g_kvnorm, g_kvnorm.reshape(KV_LORA, 1))

    xp2 = x_prompt.reshape(batch * seq, d_model)
    cos_p, sin_p = _rope_tables(N_META + jnp.arange(seq, dtype=jnp.int32))
    proj_p = _run_proj(xp2, cos_p, sin_p, seq // KEY_BLOCK, weights, KEY_BLOCK)
    n_s = n_dec * dec_seq
    n_small = -(-(n_s + N_META) // SMALL_TM) * SMALL_TM
    xs2 = x_sample.reshape(n_s, d_model)
    x_small = jnp.concatenate([xs2, meta_tokens.astype(x_prompt.dtype),
                               jnp.zeros((n_small - n_s - N_META, d_model), x_prompt.dtype)], axis=0)
    pos_small = jnp.concatenate([past_len + (jnp.arange(n_s, dtype=jnp.int32) % dec_seq),
                                 jnp.arange(n_small - n_s, dtype=jnp.int32)])
    cos_s, sin_s = _rope_tables(pos_small)
    proj_s = _run_proj(x_small, cos_s, sin_s, n_small // SMALL_TM, weights, SMALL_TM)

    (qt_p, k_p, kbf_p, v_p, vt_p, sza_p, szb_p, ga_p, gb_p, qlatt_p, qropet_p, ckv_p, kcat_p,
     ckvt_p, krope_p) = proj_p
    (qt_s, k_s, kbf_s, v_s, vt_s, sza_s, szb_s, ga_s, gb_s, qlatt_s, qropet_s, ckv_s, kcat_s,
     ckvt_s, krope_s) = proj_s
    assert n_s % SMALL_TM == 0
    meta_blk = n_s // SMALL_TM

    oa_p = _run_sb_prompt(qt_p, kbf_p, vt_p, kbf_s[n_s:n_s + SMALL_TM], vt_s[meta_blk], tri.T,
                          batch, seq)
    ob_p = _run_mla_prompt(qlatt_p, qropet_p, kcat_p, ckvt_p, kcat_s[n_s:n_s + SMALL_TM],
                           ckvt_s[meta_blk], wuvt, batch, seq)
    y_p = _run_out(xp2, oa_p, ob_p, sza_p, szb_p, ga_p, gb_p, wpa, wpb, wout, g_post, OUT_TM)

    rows = SB_HEADS * dec_seq
    q4 = qt_s[:, :n_s].T.reshape(n_dec, dec_seq, SB_WIDTH)
    qbd = jnp.tile(q4, (1, SB_HEADS, 1)).reshape(n_dec, SB_HEADS, dec_seq, SB_WIDTH)
    own = (jnp.arange(SB_WIDTH)[None, :] // SB_HEAD_DIM) == jnp.arange(SB_HEADS)[:, None]
    qbd = jnp.where(own[None, :, None, :], qbd, jnp.zeros_like(qbd)).reshape(n_dec, rows, SB_WIDTH)
    qlat_d = qlatt_s[:, :n_s].reshape(MLA_HEADS, KV_LORA, n_dec, dec_seq).transpose(2, 0, 3, 1)
    qlat_d = qlat_d.reshape(n_dec, rows, KV_LORA)
    qrope_d = qropet_s[:, :n_s].reshape(MLA_HEADS, MLA_ROPE, n_dec, dec_seq).transpose(2, 0, 3, 1)
    qrope_d = qrope_d.reshape(n_dec, rows, MLA_ROPE)

    def new_rows(a):
        a = a[:n_s].reshape(n_dec, dec_seq, a.shape[-1])
        return jnp.pad(a, ((0, 0), (0, 8 - dec_seq), (0, 0)))

    cache_kt = cache_sb_k[0].transpose(0, 2, 3, 1).reshape(n_pool, SB_WIDTH, page)
    cache_vt = cache_sb_v[0].transpose(0, 2, 3, 1).reshape(n_pool, SB_WIDTH, page)
    cache_rt = cache_mla_krope[0].transpose(0, 2, 1)
    oa_s, ob_s = _run_decode(
        page_table, qbd, qlat_d, qrope_d, new_rows(k_s), new_rows(v_s), new_rows(ckv_s),
        new_rows(krope_s), tri, wuv_all, cache_kt, cache_vt, cache_mla_ckv[0], cache_rt)
    y_s = _run_out(xs2, oa_s.reshape(n_s, SB_WIDTH), ob_s.reshape(n_s, MLA_WIDTH), sza_s[:n_s],
                   szb_s[:n_s], ga_s[:n_s], gb_s[:n_s], wpa, wpb, wout, g_post, n_s)

    def with_meta(small, big):
        width = big.shape[-1]
        meta = jnp.broadcast_to(small[n_s:n_s + N_META][None], (batch, N_META, width))
        return jnp.concatenate([meta, big.reshape(batch, seq, width)], axis=1)[None]

    lp = seq + N_META
    return (
        y_p.reshape(batch, seq, d_model),
        y_s.reshape(n_dec, dec_seq, d_model),
        with_meta(k_s, k_p).reshape(1, batch, lp, SB_HEADS, SB_HEAD_DIM),
        with_meta(v_s, v_p).reshape(1, batch, lp, SB_HEADS, SB_HEAD_DIM),
        with_meta(ckv_s, ckv_p),
        with_meta(krope_s, krope_p),
        k_s[:n_s].reshape(1, n_dec, dec_seq, SB_HEADS, SB_HEAD_DIM),
        v_s[:n_s].reshape(1, n_dec, dec_seq, SB_HEADS, SB_HEAD_DIM),
        ckv_s[:n_s].reshape(1, n_dec, dec_seq, KV_LORA),
        krope_s[:n_s].reshape(1, n_dec, dec_seq, MLA_ROPE),
    )
```

```python
import jax
import jax.numpy as jnp
from jax import lax
from jax.experimental import pallas as pl
from jax.experimental.pallas import tpu as pltpu

F32 = jnp.float32
BF16 = jnp.bfloat16

N_META = 16
SB_HEADS = 8
SB_HEAD_DIM = 64
SB_WIDTH = SB_HEADS * SB_HEAD_DIM
MLA_HEADS = 8
MLA_NOPE = 64
MLA_ROPE = 32
MLA_V = 64
MLA_WIDTH = MLA_HEADS * MLA_V
Q_LORA = 384
KV_LORA = 256
D_MODEL = 1024
MLA_SCALE = (MLA_NOPE + MLA_ROPE) ** -0.5
SB_SCALE = SB_HEAD_DIM ** -0.5
ROPE_THETA = 10000.0
RMS_EPS = 1e-6
NEG_INF = -1e30
SB_EXIT = 120.0

LANES = 128
ROPE_TILED = MLA_HEADS * MLA_ROPE
NOPE_PAD = LANES
PAIR = 2 * SB_HEAD_DIM
N_PAIRS = SB_WIDTH // PAIR

_C_Q, _C_K, _C_V, _C_ZA = 0, 512, 1024, 1536
_C_CQ = 2048
_C_CKV = _C_CQ + Q_LORA
_C_ZB = _C_CKV + KV_LORA
_C_GA = _C_ZB + MLA_WIDTH
_C_GB = _C_GA + D_MODEL
_C_KR = _C_GB + D_MODEL
_C_KRS = _C_KR + ROPE_TILED
_N_AUG = _C_KRS + ROPE_TILED
_R_ROPE = MLA_HEADS * NOPE_PAD
_R_ROPES = _R_ROPE + ROPE_TILED
_N_UQ = _R_ROPES + ROPE_TILED

VMEM_LIMIT = 56 * 1024 * 1024

KEY_BLOCK = 256
SMALL_TM = 128
OUT_TM = 512
MLA_TQ = 256
PAGES_PER_CHUNK = 16
N_SLOTS = 3
SB_PAGES_PER_CHUNK = 4


def _dot(a, b):
    return jnp.dot(a, b, preferred_element_type=F32)


def _dot_nt(a, b):
    return lax.dot_general(a, b, (((1,), (1,)), ((), ())), preferred_element_type=F32)


def _rms(x, g):
    ms = jnp.mean(x * x, axis=-1, keepdims=True)
    return x * lax.rsqrt(ms + RMS_EPS) * g


def _softplus(z):
    return jnp.maximum(z, 0.0) + jnp.log(1.0 + jnp.exp(-jnp.abs(z)))


def _sigmoid(z):
    return 1.0 / (1.0 + jnp.exp(-z))


def _split_bf16(x):
    hi = x.astype(BF16)
    lo = (x - hi.astype(F32)).astype(BF16)
    return hi, lo


def _proj_kernel(x_ref, c_ref, s_ref, ct_ref, st_ref, gpre_ref, w_ref, wqt_ref, wvt_ref, wct_ref, gq_ref,
                 wuqt_ref, wuk_ref, gkv_ref, gkvt_ref,
                 qt_ref, k_ref, kbf_ref, v_ref, vt_ref, sza_ref, szb_ref, ga_ref, gb_ref,
                 qlatt_ref, qropet_ref, ckv_ref, kcat_ref, ckvt_ref, krope_ref):
    h = _rms(x_ref[...], gpre_ref[...]).astype(BF16)

    def proj(a, b):
        return _dot(h, w_ref[:, a:b])

    qt_ref[...] = (_dot_nt(wqt_ref[...], h) * SB_SCALE).astype(BF16)
    r = proj(_C_K, _C_V)
    k_ref[...] = r
    kbf_ref[...] = r.astype(BF16)
    v_ref[...] = proj(_C_V, _C_ZA)
    vt_ref[...] = _dot_nt(wvt_ref[...], h).astype(BF16)
    r = proj(_C_ZA, _C_CQ)
    sza_ref[...] = r * _sigmoid(r)
    r = proj(_C_ZB, _C_GA)
    szb_ref[...] = r * _sigmoid(r)
    ga_ref[...] = _sigmoid(proj(_C_GA, _C_GB))
    gb_ref[...] = _sigmoid(proj(_C_GB, _C_KR))

    ckv = _rms(proj(_C_CKV, _C_ZB), gkv_ref[...])
    ckv_ref[...] = ckv
    kcat_ref[:, 0:KV_LORA] = ckv.astype(BF16)
    ct = _dot_nt(wct_ref[...], h)
    ms = jnp.mean(ct * ct, axis=0, keepdims=True)
    ckvt_ref[...] = (ct * lax.rsqrt(ms + RMS_EPS) * gkvt_ref[...]).astype(BF16)
    kr = proj(_C_KR, _C_KRS) * c_ref[...] + proj(_C_KRS, _N_AUG) * s_ref[...]
    krope_ref[...] = kr[:, 0:MLA_ROPE]
    kcat_ref[:, KV_LORA:KV_LORA + ROPE_TILED] = kr.astype(BF16)

    cq = _rms(proj(_C_CQ, _C_CKV), gq_ref[...]).astype(BF16)
    qmt = _dot_nt(wuqt_ref[...], cq)
    for hd in range(MLA_HEADS):
        qn = qmt[hd * NOPE_PAD:(hd + 1) * NOPE_PAD, :].astype(BF16)
        ql = _dot(wuk_ref[hd], qn) * MLA_SCALE
        qlatt_ref[hd * KV_LORA:(hd + 1) * KV_LORA, :] = ql.astype(BF16)
    qr = qmt[_R_ROPE:_R_ROPES, :] * ct_ref[...] + qmt[_R_ROPES:_N_UQ, :] * st_ref[...]
    qropet_ref[...] = (qr * MLA_SCALE).astype(BF16)


def _run_proj(x2d, cos_t, sin_t, n_table_blocks, weights, tm):
    gpre, w_aug, wqt, wvt, wct, gq, wuqt, wuk_pad, gkv, gkvt = weights
    rows = x2d.shape[0]
    nblk = rows // tm
    row = lambda t: (t, 0)
    col = lambda t: (0, t)
    tab = lambda t: (t % n_table_blocks, 0)
    tabt = lambda t: (0, t % n_table_blocks)
    const2 = lambda t: (0, 0)
    const3 = lambda t: (0, 0, 0)

    def out(width, dtype):
        return jax.ShapeDtypeStruct((rows, width), dtype), pl.BlockSpec((tm, width), row)

    def out_t(height, dtype):
        return jax.ShapeDtypeStruct((height, rows), dtype), pl.BlockSpec((height, tm), col)

    def out_blocked(height, dtype):
        return (jax.ShapeDtypeStruct((nblk, height, tm), dtype),
                pl.BlockSpec((None, height, tm), lambda t: (t, 0, 0)))

    outs = [out_t(SB_WIDTH, BF16), out(SB_WIDTH, F32), out(SB_WIDTH, BF16), out(SB_WIDTH, F32),
            out_blocked(SB_WIDTH, BF16), out(SB_WIDTH, F32), out(MLA_WIDTH, F32), out(D_MODEL, F32),
            out(D_MODEL, F32), out_t(MLA_HEADS * KV_LORA, BF16), out_t(ROPE_TILED, BF16),
            out(KV_LORA, F32), out(KV_LORA + ROPE_TILED, BF16), out_blocked(KV_LORA, BF16),
            out(MLA_ROPE, F32)]
    cos_tt, sin_tt = cos_t.T, sin_t.T
    return pl.pallas_call(
        _proj_kernel,
        grid=(nblk,),
        in_specs=[
            pl.BlockSpec((tm, D_MODEL), row),
            pl.BlockSpec((tm, ROPE_TILED), tab),
            pl.BlockSpec((tm, ROPE_TILED), tab),
            pl.BlockSpec((ROPE_TILED, tm), tabt),
            pl.BlockSpec((ROPE_TILED, tm), tabt),
            pl.BlockSpec((1, D_MODEL), const2),
            pl.BlockSpec(w_aug.shape, const2),
            pl.BlockSpec(wqt.shape, const2),
            pl.BlockSpec(wvt.shape, const2),
            pl.BlockSpec(wct.shape, const2),
            pl.BlockSpec((1, Q_LORA), const2),
            pl.BlockSpec(wuqt.shape, const2),
            pl.BlockSpec(wuk_pad.shape, const3),
            pl.BlockSpec((1, KV_LORA), const2),
            pl.BlockSpec((KV_LORA, 1), const2),
        ],
        out_specs=[o[1] for o in outs],
        out_shape=[o[0] for o in outs],
        compiler_params=pltpu.CompilerParams(
            dimension_semantics=("arbitrary",), vmem_limit_bytes=VMEM_LIMIT),
        name="proj",
    )(x2d, cos_t, sin_t, cos_tt, sin_tt, gpre, w_aug, wqt, wvt, wct, gq, wuqt, wuk_pad, gkv, gkvt)


def _sb_prompt_kernel(qt_ref, k_ref, vt_ref, km_ref, vtm_ref, trit_ref, o_ref, q2_ref, carry_ref, acc_ref):
    qi = pl.program_id(1)
    t = KEY_BLOCK
    trit = trit_ref[...]
    trit2 = jnp.concatenate([trit, trit], axis=1)
    top = lax.broadcasted_iota(jnp.int32, (PAIR, t), 0) < SB_HEAD_DIM
    pairs = range(N_PAIRS)

    for p in pairs:
        qp = qt_ref[p * PAIR:(p + 1) * PAIR, :]
        zero = jnp.zeros_like(qp)
        q2_ref[p, :, 0:t] = jnp.where(top, qp, zero)
        q2_ref[p, :, t:2 * t] = jnp.where(top, zero, qp)
    carry_ref[...] = jnp.zeros(carry_ref.shape, F32)
    acc_ref[...] = jnp.zeros(acc_ref.shape, F32)

    def sweep(kb_of, vtb_of, tr2, mask):
        zs = [_dot(kb_of(p), q2_ref[p]) for p in pairs]
        sps = [_softplus(z) for z in zs]
        kept = sps if mask is None else [jnp.where(mask, sp, 0.0) for sp in sps]
        splits = [_split_bf16(sp) for sp in kept]
        locs = [_dot(tr2, jnp.concatenate([hi, lo], axis=0)) for hi, lo in splits]
        ws = [jnp.exp((z - sp) - (loc + carry_ref[p])) for p, z, sp, loc in zip(pairs, zs, sps, locs)]
        if mask is not None:
            ws = [jnp.where(mask, w, 0.0) for w in ws]
        for p in pairs:
            acc_ref[p] += _dot(vtb_of(p), ws[p].astype(BF16))
            carry_ref[p] += locs[p][0:1, :] + kept[p][0:1, :]

    def own_block(kb, mask):
        start = pl.multiple_of(kb * t, t)
        sweep(lambda p: k_ref[pl.ds(start, t), p * PAIR:(p + 1) * PAIR],
              lambda p: vt_ref[kb, p * PAIR:(p + 1) * PAIR, :], trit2, mask)

    k_in = lax.broadcasted_iota(jnp.int32, (t, 2 * t), 0)
    q_in = lax.broadcasted_iota(jnp.int32, (t, 2 * t), 1) & (t - 1)
    own_block(qi, k_in < q_in)

    def more(state):
        j, lowest = state
        return jnp.logical_and(j < qi, lowest < SB_EXIT)

    def body(state):
        j, _ = state
        own_block(qi - 1 - j, None)
        return j + 1, jnp.min(carry_ref[...])

    _, lowest = lax.while_loop(more, body, (jnp.int32(0), jnp.min(carry_ref[...])))

    @pl.when(lowest < SB_EXIT)
    def _():
        tm = km_ref.shape[0]
        m_in = lax.broadcasted_iota(jnp.int32, (tm, 2 * t), 0)
        tritm = trit_ref[0:tm, 0:tm]
        sweep(lambda p: km_ref[:, p * PAIR:(p + 1) * PAIR], lambda p: vtm_ref[p * PAIR:(p + 1) * PAIR, :],
              jnp.concatenate([tritm, tritm], axis=1), m_in < N_META)

    for p in pairs:
        acc = acc_ref[p]
        o_ref[:, p * PAIR:(p + 1) * PAIR] = jnp.where(top, acc[:, 0:t], acc[:, t:2 * t]).T


def _run_sb_prompt(qt, kbf, vt3, k_meta, vt_meta, trit, batch, seq):
    t = KEY_BLOCK
    nq = seq // t
    return pl.pallas_call(
        _sb_prompt_kernel,
        grid=(batch, nq),
        in_specs=[
            pl.BlockSpec((SB_WIDTH, t), lambda i, j: (0, i * nq + j)),
            pl.BlockSpec((seq, SB_WIDTH), lambda i, j: (i, 0)),
            pl.BlockSpec((nq, SB_WIDTH, t), lambda i, j: (i, 0, 0)),
            pl.BlockSpec(k_meta.shape, lambda i, j: (0, 0)),
            pl.BlockSpec(vt_meta.shape, lambda i, j: (0, 0)),
            pl.BlockSpec(trit.shape, lambda i, j: (0, 0)),
        ],
        out_specs=pl.BlockSpec((t, SB_WIDTH), lambda i, j: (i * nq + j, 0)),
        out_shape=jax.ShapeDtypeStruct((batch * seq, SB_WIDTH), F32),
        scratch_shapes=[
            pltpu.VMEM((N_PAIRS, PAIR, 2 * t), BF16),
            pltpu.VMEM((N_PAIRS, 1, 2 * t), F32),
            pltpu.VMEM((N_PAIRS, PAIR, 2 * t), F32),
        ],
        compiler_params=pltpu.CompilerParams(
            dimension_semantics=("arbitrary", "arbitrary"), vmem_limit_bytes=VMEM_LIMIT),
        name="sb_prompt",
    )(qt, kbf, vt3, k_meta, vt_meta, trit)


def _mla_prompt_kernel(qlatt_ref, qropet_ref, kcat_ref, ckvt_ref, kmeta_ref, ctmeta_ref, wuvt_ref,
                       o_ref, qcat_ref, m_ref, l_ref, acc_ref):
    qi = pl.program_id(1)
    tq, tk = MLA_TQ, KEY_BLOCK
    mq = MLA_HEADS * tq

    sub = lax.broadcasted_iota(jnp.int32, (ROPE_TILED, tq), 0)
    qrt = qropet_ref[...]
    for hd in range(MLA_HEADS):
        qcat_ref[0:KV_LORA, hd * tq:(hd + 1) * tq] = qlatt_ref[hd * KV_LORA:(hd + 1) * KV_LORA, :]
        own = (sub >= hd * MLA_ROPE) & (sub < (hd + 1) * MLA_ROPE)
        qcat_ref[KV_LORA:, hd * tq:(hd + 1) * tq] = jnp.where(own, qrt, jnp.zeros_like(qrt))
    qcat = qcat_ref[...]

    m_ref[...] = jnp.full(m_ref.shape, NEG_INF, F32)
    l_ref[...] = jnp.zeros(l_ref.shape, F32)
    acc_ref[...] = jnp.zeros(acc_ref.shape, F32)

    def step(kc, ct, mask):
        s = _dot(kc, qcat)
        if mask is not None:
            s = jnp.where(mask, s, NEG_INF)
        m_old = m_ref[...]
        m_new = jnp.maximum(m_old, jnp.max(s, axis=0, keepdims=True))
        alpha = jnp.exp(m_old - m_new)
        p = jnp.exp(s - m_new)
        l_ref[...] = alpha * l_ref[...] + jnp.sum(p, axis=0, keepdims=True)
        acc_ref[...] = alpha * acc_ref[...] + _dot(ct, p.astype(BF16))
        m_ref[...] = m_new

    jd = (qi * tq) // tk
    r_in = lax.broadcasted_iota(jnp.int32, (tk, mq), 0)
    c_in = lax.broadcasted_iota(jnp.int32, (tk, mq), 1) & (tq - 1)
    off = qi * tq - jd * tk
    step(kcat_ref[pl.ds(pl.multiple_of(jd * tk, tk), tk), :], ckvt_ref[jd], (r_in - c_in) <= off)

    def body(j, _):
        step(kcat_ref[pl.ds(pl.multiple_of(j * tk, tk), tk), :], ckvt_ref[j], None)
        return 0

    lax.fori_loop(0, jd, body, 0)

    tmeta = kmeta_ref.shape[0]
    mrow = lax.broadcasted_iota(jnp.int32, (tmeta, mq), 0)
    step(kmeta_ref[...], ctmeta_ref[...], mrow < N_META)

    o_lat = (acc_ref[...] / l_ref[...]).astype(BF16)
    full = _dot(wuvt_ref[...], o_lat)
    row_head = lax.broadcasted_iota(jnp.int32, (MLA_WIDTH, tq), 0) // MLA_V
    out_t = jnp.zeros((MLA_WIDTH, tq), F32)
    for hd in range(MLA_HEADS):
        out_t = out_t + jnp.where(row_head == hd, full[:, hd * tq:(hd + 1) * tq], 0.0)
    o_ref[...] = out_t.T


def _run_mla_prompt(qlatt, qropet, kcat, ckvt3, kcat_meta, ckvt_meta, wuvt, batch, seq):
    tq, tk = MLA_TQ, KEY_BLOCK
    nq = seq // tq
    nk = seq // tk
    mq = MLA_HEADS * tq
    return pl.pallas_call(
        _mla_prompt_kernel,
        grid=(batch, nq),
        in_specs=[
            pl.BlockSpec((qlatt.shape[0], tq), lambda i, j: (0, i * nq + j)),
            pl.BlockSpec((qropet.shape[0], tq), lambda i, j: (0, i * nq + j)),
            pl.BlockSpec((seq, kcat.shape[1]), lambda i, j: (i, 0)),
            pl.BlockSpec((nk, KV_LORA, tk), lambda i, j: (i, 0, 0)),
            pl.BlockSpec(kcat_meta.shape, lambda i, j: (0, 0)),
            pl.BlockSpec(ckvt_meta.shape, lambda i, j: (0, 0)),
            pl.BlockSpec(wuvt.shape, lambda i, j: (0, 0)),
        ],
        out_specs=pl.BlockSpec((tq, MLA_WIDTH), lambda i, j: (i * nq + j, 0)),
        out_shape=jax.ShapeDtypeStruct((batch * seq, MLA_WIDTH), F32),
        scratch_shapes=[
            pltpu.VMEM((KV_LORA + ROPE_TILED, mq), BF16),
            pltpu.VMEM((1, mq), F32),
            pltpu.VMEM((1, mq), F32),
            pltpu.VMEM((KV_LORA, mq), F32),
        ],
        compiler_params=pltpu.CompilerParams(
            dimension_semantics=("arbitrary", "arbitrary"), vmem_limit_bytes=VMEM_LIMIT),
        name="mla_prompt",
    )(qlatt, qropet, kcat, ckvt3, kcat_meta, ckvt_meta, wuvt)


def _rev_cumsum_blocks(lk, tri):
    rows, width = lk.shape
    cb = tri.shape[0]
    nb = width // cb
    blocks = [lk[:, b * cb:(b + 1) * cb] for b in range(nb)]
    stacked = blocks[0] if nb == 1 else jnp.concatenate(blocks, axis=0)
    hi, lo = _split_bf16(stacked)
    loc = _dot(hi, tri) + _dot(lo, tri)
    tots = jnp.sum(stacked, axis=-1, keepdims=True)
    outs = [None] * nb
    off = jnp.zeros((rows, 1), F32)
    for b in range(nb - 1, -1, -1):
        outs[b] = loc[b * rows:(b + 1) * rows, :] + off
        off = off + tots[b * rows:(b + 1) * rows, :]
    later = outs[0] if nb == 1 else jnp.concatenate(outs, axis=1)
    return later, off


def _sb_weights(z, tri, mask, carry):
    sp = _softplus(z)
    lk = -sp if mask is None else jnp.where(mask, -sp, 0.0)
    later, total = _rev_cumsum_blocks(lk, tri)
    w = jnp.exp((z - sp) + (later + carry))
    if mask is not None:
        w = jnp.where(mask, w, 0.0)
    return w.astype(BF16), carry + total


def _softmax_update(s, mask, m, l):
    if mask is not None:
        s = jnp.where(mask, s, NEG_INF)
    m_new = jnp.maximum(m, jnp.max(s, axis=-1, keepdims=True))
    alpha = jnp.exp(m - m_new)
    p = jnp.exp(s - m_new)
    return p.astype(BF16), alpha, m_new, alpha * l + jnp.sum(p, axis=-1, keepdims=True)


def _head_diag_rows(full):
    rows, width = full.shape
    nq = rows // SB_HEADS
    row_head = lax.broadcasted_iota(jnp.int32, (rows, width), 0) // nq
    col_head = lax.broadcasted_iota(jnp.int32, (rows, width), 1) // SB_HEAD_DIM
    kept = jnp.where(row_head == col_head, full, 0.0)
    out = kept[0:nq, :]
    for hd in range(1, SB_HEADS):
        out = out + kept[hd * nq:(hd + 1) * nq, :]
    return out


def _decode_kernel(pt_ref, qbd_ref, qlat_ref, qrope_ref, knew_ref, vnew_ref, cnew_ref, rnew_ref,
                   tri_ref, wuv_ref, ck_hbm, cv_hbm, cc_hbm, cr_hbm, oa_ref, ob_ref,
                   kbuf, vbuf, cbuf, rbuf, acct_ref, sem, sem_sb):
    s = pl.program_id(0)
    n_seq = pl.num_programs(0)
    n_pages = pt_ref.shape[1]
    g = PAGES_PER_CHUNK
    n_chunks = n_pages // g
    page = cbuf.shape[1] // g
    gs = SB_PAGES_PER_CHUNK
    n_sb_chunks = n_pages // gs

    def copies(seq, chunk, slot):
        out = []
        for p in range(g):
            pg = pt_ref[seq, chunk * g + p]
            span = pl.ds(p * page, page)
            out.append(pltpu.make_async_copy(cc_hbm.at[pg], cbuf.at[slot, span, :], sem.at[slot]))
            out.append(pltpu.make_async_copy(cr_hbm.at[pg], rbuf.at[slot, :, span], sem.at[slot]))
        return out

    def sb_copies(seq, k, slot):
        out = []
        for p in range(gs):
            pg = pt_ref[seq, n_pages - (k + 1) * gs + p]
            span = pl.ds(p * page, page)
            out.append(pltpu.make_async_copy(ck_hbm.at[pg], kbuf.at[slot, :, span], sem_sb.at[slot]))
            out.append(pltpu.make_async_copy(cv_hbm.at[pg], vbuf.at[slot, :, span], sem_sb.at[slot]))
        return out

    def start_all(cs):
        for c in cs:
            c.start()

    def wait_all(cs):
        for c in cs:
            c.wait()

    def start_global(gidx):
        seq = lax.div(gidx, n_chunks)
        start_all(copies(seq, n_chunks - 1 - lax.rem(gidx, n_chunks), lax.rem(gidx, N_SLOTS)))

    @pl.when(s == 0)
    def _():
        start_all(sb_copies(0, 0, 0))
        for ahead in range(N_SLOTS - 1):
            start_all(copies(ahead // n_chunks, n_chunks - 1 - ahead % n_chunks, ahead % N_SLOTS))

    @pl.when(s + 1 < n_seq)
    def _():
        start_all(sb_copies(s + 1, 0, (s + 1) & 1))

    qbd = qbd_ref[...]
    qlat = qlat_ref[...]
    qrope = qrope_ref[...]
    tri = tri_ref[...]
    rows = qbd.shape[0]
    nq = rows // SB_HEADS

    n_new = knew_ref.shape[0]

    def pad_new(ref):
        x = ref[...]
        return jnp.concatenate([x, jnp.zeros((page - n_new, x.shape[1]), x.dtype)], axis=0).astype(BF16)

    q_idx = lax.broadcasted_iota(jnp.int32, (rows, page), 0) & (nq - 1)
    k_idx = lax.broadcasted_iota(jnp.int32, (rows, page), 1)
    w, carry = _sb_weights(_dot_nt(qbd, pad_new(knew_ref)), tri_ref[0:page, 0:page], k_idx < q_idx,
                           jnp.zeros((rows, 1), F32))
    oa_new = _dot(w, pad_new(vnew_ref))
    cnew = pad_new(cnew_ref)
    p, _, m, l = _softmax_update(_dot_nt(qlat, cnew) + _dot_nt(qrope, pad_new(rnew_ref)),
                                 k_idx <= q_idx, jnp.full((rows, 1), NEG_INF, F32),
                                 jnp.zeros((rows, 1), F32))
    acc_b = _dot(p, cnew)
    acct_ref[...] = jnp.zeros(acct_ref.shape, F32)

    def sb_chunk(slot, carry):
        w, carry = _sb_weights(_dot(qbd, kbuf[slot].astype(BF16)), tri, None, carry)
        acct_ref[:, 0:rows] += _dot_nt(vbuf[slot].astype(BF16), w)
        return carry

    first = s & 1
    wait_all(sb_copies(s, 0, first))
    carry = sb_chunk(first, carry)

    def sb_more(state):
        k, carry = state
        return jnp.logical_and(k < n_sb_chunks, jnp.max(carry) > -SB_EXIT)

    def sb_body(state):
        k, carry = state
        start_all(sb_copies(s, k, 2))
        wait_all(sb_copies(s, k, 2))
        return k + 1, sb_chunk(2, carry)

    lax.while_loop(sb_more, sb_body, (jnp.int32(1), carry))

    def body(k, state):
        m, l, acc_b = state
        chunk = n_chunks - 1 - k
        gidx = s * n_chunks + k
        slot = lax.rem(gidx, N_SLOTS)

        @pl.when(gidx + (N_SLOTS - 1) < n_seq * n_chunks)
        def _():
            start_global(gidx + (N_SLOTS - 1))

        wait_all(copies(s, chunk, slot))
        cc = cbuf[slot].astype(BF16)
        p, alpha, m, l = _softmax_update(_dot_nt(qlat, cc) + _dot(qrope, rbuf[slot].astype(BF16)),
                                         None, m, l)
        acc_b = alpha * acc_b + _dot(p, cc)
        return m, l, acc_b

    m, l, acc_b = lax.fori_loop(0, n_chunks, body, (m, l, acc_b))

    oa_ref[...] = _head_diag_rows(acct_ref[...].T[0:rows, :] + oa_new)
    o_lat = (acc_b / l).astype(BF16)
    ob_ref[...] = _head_diag_rows(_dot(o_lat, wuv_ref[...]))


def _run_decode(page_table, qbd, qlat, qrope, knew, vnew, cnew, rnew, tri, wuv_all,
                cache_kt, cache_vt, cache_c, cache_rt):
    n_seq, n_pages = page_table.shape
    page = cache_c.shape[1]
    rows = qbd.shape[1]
    nq = rows // SB_HEADS
    n_new = knew.shape[1]
    tok = PAGES_PER_CHUNK * page
    sb_tok = SB_PAGES_PER_CHUNK * page
    per_seq = lambda i, pt: (i, 0, 0)
    const2 = lambda i, pt: (0, 0)
    grid_spec = pltpu.PrefetchScalarGridSpec(
        num_scalar_prefetch=1,
        grid=(n_seq,),
        in_specs=[
            pl.BlockSpec((None, rows, SB_WIDTH), per_seq),
            pl.BlockSpec((None, rows, KV_LORA), per_seq),
            pl.BlockSpec((None, rows, MLA_ROPE), per_seq),
            pl.BlockSpec((None, n_new, SB_WIDTH), per_seq),
            pl.BlockSpec((None, n_new, SB_WIDTH), per_seq),
            pl.BlockSpec((None, n_new, KV_LORA), per_seq),
            pl.BlockSpec((None, n_new, MLA_ROPE), per_seq),
            pl.BlockSpec(tri.shape, const2),
            pl.BlockSpec(wuv_all.shape, const2),
            pl.BlockSpec(memory_space=pl.ANY),
            pl.BlockSpec(memory_space=pl.ANY),
            pl.BlockSpec(memory_space=pl.ANY),
            pl.BlockSpec(memory_space=pl.ANY),
        ],
        out_specs=[
            pl.BlockSpec((None, nq, SB_WIDTH), per_seq),
            pl.BlockSpec((None, nq, MLA_WIDTH), per_seq),
        ],
        scratch_shapes=[
            pltpu.VMEM((3, SB_WIDTH, sb_tok), F32),
            pltpu.VMEM((3, SB_WIDTH, sb_tok), F32),
            pltpu.VMEM((N_SLOTS, tok, KV_LORA), F32),
            pltpu.VMEM((N_SLOTS, MLA_ROPE, tok), F32),
            pltpu.VMEM((SB_WIDTH, LANES), F32),
            pltpu.SemaphoreType.DMA((N_SLOTS,)),
            pltpu.SemaphoreType.DMA((3,)),
        ],
    )
    return pl.pallas_call(
        _decode_kernel,
        grid_spec=grid_spec,
        out_shape=[jax.ShapeDtypeStruct((n_seq, nq, SB_WIDTH), F32),
                   jax.ShapeDtypeStruct((n_seq, nq, MLA_WIDTH), F32)],
        compiler_params=pltpu.CompilerParams(
            dimension_semantics=("arbitrary",), vmem_limit_bytes=VMEM_LIMIT),
        name="decode",
    )(page_table, qbd, qlat, qrope, knew, vnew, cnew, rnew, tri, wuv_all,
      cache_kt, cache_vt, cache_c, cache_rt)


def _out_kernel(x_ref, oa_ref, ob_ref, sza_ref, szb_ref, ga_ref, gb_ref, wpa_ref, wpb_ref,
                wout_ref, gpost_ref, y_ref):
    a = (oa_ref[...] * sza_ref[...]).astype(BF16)
    b = (ob_ref[...] * szb_ref[...]).astype(BF16)
    merged = ga_ref[...] * _dot(a, wpa_ref[...]) + gb_ref[...] * _dot(b, wpb_ref[...])
    t = _dot(merged.astype(BF16), wout_ref[...])
    y_ref[...] = x_ref[...] + _rms(t, gpost_ref[...])


def _run_out(x2d, oa, ob, sza, szb, ga, gb, wpa, wpb, wout, gpost, tm):
    rows = x2d.shape[0]
    row = lambda t: (t, 0)
    const2 = lambda t: (0, 0)
    return pl.pallas_call(
        _out_kernel,
        grid=(rows // tm,),
        in_specs=[
            pl.BlockSpec((tm, D_MODEL), row),
            pl.BlockSpec((tm, SB_WIDTH), row),
            pl.BlockSpec((tm, MLA_WIDTH), row),
            pl.BlockSpec((tm, SB_WIDTH), row),
            pl.BlockSpec((tm, MLA_WIDTH), row),
            pl.BlockSpec((tm, D_MODEL), row),
            pl.BlockSpec((tm, D_MODEL), row),
            pl.BlockSpec(wpa.shape, const2),
            pl.BlockSpec(wpb.shape, const2),
            pl.BlockSpec(wout.shape, const2),
            pl.BlockSpec((1, D_MODEL), const2),
        ],
        out_specs=pl.BlockSpec((tm, D_MODEL), row),
        out_shape=jax.ShapeDtypeStruct((rows, D_MODEL), F32),
        compiler_params=pltpu.CompilerParams(
            dimension_semantics=("arbitrary",), vmem_limit_bytes=VMEM_LIMIT),
        name="out_mix",
    )(x2d, oa, ob, sza, szb, ga, gb, wpa, wpb, wout, gpost)


def _rope_tables(pos):
    half = MLA_ROPE // 2
    inv = ROPE_THETA ** (-jnp.arange(half, dtype=jnp.float32) * 2.0 / MLA_ROPE)
    ang = pos.astype(jnp.float32)[:, None] * inv[None, :]
    cos, sin = jnp.cos(ang), jnp.sin(ang)
    cos_t = jnp.repeat(cos, 2, axis=1)
    sin_t = jnp.stack([-sin, sin], axis=-1).reshape(pos.shape[0], MLA_ROPE)
    return jnp.tile(cos_t, (1, MLA_HEADS)), jnp.tile(sin_t, (1, MLA_HEADS))


def kernel(x_prompt, x_sample, cache_sb_k, cache_sb_v, cache_mla_ckv, cache_mla_krope, page_table,
           meta_tokens, g_pre, w_in, g_qnorm, w_uq, g_kvnorm, w_uk, w_uv, w_proj_a, w_proj_b,
           w_out, g_post):
    assert g_pre.shape[0] == 1, "single layer step"
    batch, seq, d_model = x_prompt.shape
    n_dec, dec_seq, _ = x_sample.shape
    n_pool, page = cache_sb_k.shape[1], cache_sb_k.shape[2]
    past_len = page_table.shape[1] * page
    pair_swap = jnp.arange(MLA_ROPE) ^ 1

    w = w_in[0]
    o_ckv = 4 * SB_WIDTH + Q_LORA
    o_kr = o_ckv + KV_LORA
    w_kr = w[:, o_kr:o_kr + MLA_ROPE]
    w_aug = jnp.concatenate(
        [w[:, :o_kr], w[:, o_kr + MLA_ROPE:], jnp.tile(w_kr, (1, MLA_HEADS)),
         jnp.tile(w_kr[:, pair_swap], (1, MLA_HEADS))], axis=1).astype(BF16)
    assert w_aug.shape[1] == _N_AUG
    wqt = w[:, :SB_WIDTH].T.astype(BF16)
    wvt = w[:, 2 * SB_WIDTH:3 * SB_WIDTH].T.astype(BF16)
    wct = w[:, o_ckv:o_ckv + KV_LORA].T.astype(BF16)
    wq = w_uq[0].reshape(Q_LORA, MLA_HEADS, MLA_NOPE + MLA_ROPE)
    wq_nope = jnp.pad(wq[:, :, :MLA_NOPE], ((0, 0), (0, 0), (0, NOPE_PAD - MLA_NOPE)))
    wq_rope = wq[:, :, MLA_NOPE:]
    wuqt = jnp.concatenate(
        [wq_nope.reshape(Q_LORA, -1), wq_rope.reshape(Q_LORA, -1),
         wq_rope[:, :, pair_swap].reshape(Q_LORA, -1)], axis=1).T.astype(BF16)
    assert wuqt.shape[0] == _N_UQ
    wuk_pad = jnp.pad(w_uk[0], ((0, 0), (0, 0), (0, NOPE_PAD - MLA_NOPE))).astype(BF16)
    wuv_all = jnp.swapaxes(w_uv[0], 0, 1).reshape(KV_LORA, MLA_WIDTH).astype(BF16)
    wuvt = wuv_all.T
    wpa = w_proj_a[0].astype(BF16)
    wpb = w_proj_b[0].astype(BF16)
    wout = w_out[0].astype(BF16)
    idx = jnp.arange(KEY_BLOCK)
    tri = (idx[:, None] > idx[None, :]).astype(BF16)
    weights = (g_pre, w_aug, wqt, wvt, wct, g_qnorm, wuqt, wuk_pad, g_kvnorm, g_kvnorm.reshape(KV_LORA, 1))

    xp2 = x_prompt.reshape(batch * seq, d_model)
    cos_p, sin_p = _rope_tables(N_META + jnp.arange(seq, dtype=jnp.int32))
    proj_p = _run_proj(xp2, cos_p, sin_p, seq // KEY_BLOCK, weights, KEY_BLOCK)
    n_s = n_dec * dec_seq
    n_small = -(-(n_s + N_META) // SMALL_TM) * SMALL_TM
    xs2 = x_sample.reshape(n_s, d_model)
    x_small = jnp.concatenate([xs2, meta_tokens.astype(x_prompt.dtype),
                               jnp.zeros((n_small - n_s - N_META, d_model), x_prompt.dtype)], axis=0)
    pos_small = jnp.concatenate([past_len + (jnp.arange(n_s, dtype=jnp.int32) % dec_seq),
                                 jnp.arange(n_small - n_s, dtype=jnp.int32)])
    cos_s, sin_s = _rope_tables(pos_small)
    proj_s = _run_proj(x_small, cos_s, sin_s, n_small // SMALL_TM, weights, SMALL_TM)

    (qt_p, k_p, kbf_p, v_p, vt_p, sza_p, szb_p, ga_p, gb_p, qlatt_p, qropet_p, ckv_p, kcat_p,
     ckvt_p, krope_p) = proj_p
    (qt_s, k_s, kbf_s, v_s, vt_s, sza_s, szb_s, ga_s, gb_s, qlatt_s, qropet_s, ckv_s, kcat_s,
     ckvt_s, krope_s) = proj_s
    assert n_s % SMALL_TM == 0
    meta_blk = n_s // SMALL_TM

    oa_p = _run_sb_prompt(qt_p, kbf_p, vt_p, kbf_s[n_s:n_s + SMALL_TM], vt_s[meta_blk], tri.T,
                          batch, seq)
    ob_p = _run_mla_prompt(qlatt_p, qropet_p, kcat_p, ckvt_p, kcat_s[n_s:n_s + SMALL_TM],
                           ckvt_s[meta_blk], wuvt, batch, seq)
    y_p = _run_out(xp2, oa_p, ob_p, sza_p, szb_p, ga_p, gb_p, wpa, wpb, wout, g_post, OUT_TM)

    rows = SB_HEADS * dec_seq
    q4 = qt_s[:, :n_s].T.reshape(n_dec, dec_seq, SB_WIDTH)
    qbd = jnp.tile(q4, (1, SB_HEADS, 1)).reshape(n_dec, SB_HEADS, dec_seq, SB_WIDTH)
    own = (jnp.arange(SB_WIDTH)[None, :] // SB_HEAD_DIM) == jnp.arange(SB_HEADS)[:, None]
    qbd = jnp.where(own[None, :, None, :], qbd, jnp.zeros_like(qbd)).reshape(n_dec, rows, SB_WIDTH)
    qlat_d = qlatt_s[:, :n_s].reshape(MLA_HEADS, KV_LORA, n_dec, dec_seq).transpose(2, 0, 3, 1)
    qlat_d = qlat_d.reshape(n_dec, rows, KV_LORA)
    qrope_d = qropet_s[:, :n_s].reshape(MLA_HEADS, MLA_ROPE, n_dec, dec_seq).transpose(2, 0, 3, 1)
    qrope_d = qrope_d.reshape(n_dec, rows, MLA_ROPE)

    def new_rows(a):
        a = a[:n_s].reshape(n_dec, dec_seq, a.shape[-1])
        return jnp.pad(a, ((0, 0), (0, 8 - dec_seq), (0, 0)))

    cache_kt = cache_sb_k[0].transpose(0, 2, 3, 1).reshape(n_pool, SB_WIDTH, page)
    cache_vt = cache_sb_v[0].transpose(0, 2, 3, 1).reshape(n_pool, SB_WIDTH, page)
    cache_rt = cache_mla_krope[0].transpose(0, 2, 1)
    oa_s, ob_s = _run_decode(
        page_table, qbd, qlat_d, qrope_d, new_rows(k_s), new_rows(v_s), new_rows(ckv_s),
        new_rows(krope_s), tri, wuv_all, cache_kt, cache_vt, cache_mla_ckv[0], cache_rt)
    y_s = _run_out(xs2, oa_s.reshape(n_s, SB_WIDTH), ob_s.reshape(n_s, MLA_WIDTH), sza_s[:n_s],
                   szb_s[:n_s], ga_s[:n_s], gb_s[:n_s], wpa, wpb, wout, g_post, n_s)

    def with_meta(small, big):
        width = big.shape[-1]
        meta = jnp.broadcast_to(small[n_s:n_s + N_META][None], (batch, N_META, width))
        return jnp.concatenate([meta, big.reshape(batch, seq, width)], axis=1)[None]

    lp = seq + N_META
    return (
        y_p.reshape(batch, seq, d_model),
        y_s.reshape(n_dec, dec_seq, d_model),
        with_meta(k_s, k_p).reshape(1, batch, lp, SB_HEADS, SB_HEAD_DIM),
        with_meta(v_s, v_p).reshape(1, batch, lp, SB_HEADS, SB_HEAD_DIM),
        with_meta(ckv_s, ckv_p),
        with_meta(krope_s, krope_p),
        k_s[:n_s].reshape(1, n_dec, dec_seq, SB_HEADS, SB_HEAD_DIM),
        v_s[:n_s].reshape(1, n_dec, dec_seq, SB_HEADS, SB_HEAD_DIM),
        ckv_s[:n_s].reshape(1, n_dec, dec_seq, KV_LORA),
        krope_s[:n_s].reshape(1, n_dec, dec_seq, MLA_ROPE),
    )
```

```python
import jax
import jax.numpy as jnp
from jax import lax
from jax.experimental import pallas as pl
from jax.experimental.pallas import tpu as pltpu

F32 = jnp.float32
BF16 = jnp.bfloat16

N_META = 16
SB_HEADS = 8
SB_HEAD_DIM = 64
SB_WIDTH = SB_HEADS * SB_HEAD_DIM
MLA_HEADS = 8
MLA_NOPE = 64
MLA_ROPE = 32
MLA_V = 64
MLA_WIDTH = MLA_HEADS * MLA_V
Q_LORA = 384
KV_LORA = 256
D_MODEL = 1024
MLA_SCALE = (MLA_NOPE + MLA_ROPE) ** -0.5
SB_SCALE = SB_HEAD_DIM ** -0.5
ROPE_THETA = 10000.0
RMS_EPS = 1e-6
NEG_INF = -1e30
SB_EXIT = 120.0

LANES = 128
ROPE_TILED = MLA_HEADS * MLA_ROPE
NOPE_PAD = LANES
PAIR = 2 * SB_HEAD_DIM
N_PAIRS = SB_WIDTH // PAIR

_C_Q, _C_K, _C_V, _C_ZA = 0, 512, 1024, 1536
_C_CQ = 2048
_C_CKV = _C_CQ + Q_LORA
_C_ZB = _C_CKV + KV_LORA
_C_GA = _C_ZB + MLA_WIDTH
_C_GB = _C_GA + D_MODEL
_C_KR = _C_GB + D_MODEL
_C_KRS = _C_KR + ROPE_TILED
_N_AUG = _C_KRS + ROPE_TILED
_R_ROPE = MLA_HEADS * NOPE_PAD
_R_ROPES = _R_ROPE + ROPE_TILED
_N_UQ = _R_ROPES + ROPE_TILED

VMEM_LIMIT = 56 * 1024 * 1024

KEY_BLOCK = 256
PROJ_TM = 512
SMALL_TM = 128
OUT_TM = 512
MLA_TQ = 256
PAGES_PER_CHUNK = 32
N_SLOTS = 3
SB_PAGES_PER_CHUNK = 4


def _dot(a, b):
    return jnp.dot(a, b, preferred_element_type=F32)


def _dot_nt(a, b):
    return lax.dot_general(a, b, (((1,), (1,)), ((), ())), preferred_element_type=F32)


def _rms(x, g):
    ms = jnp.mean(x * x, axis=-1, keepdims=True)
    return x * lax.rsqrt(ms + RMS_EPS) * g


def _softplus(z):
    return jnp.maximum(z, 0.0) + jnp.log(1.0 + jnp.exp(-jnp.abs(z)))


def _sigmoid(z):
    return 1.0 / (1.0 + jnp.exp(-z))


def _split_bf16(x):
    hi = x.astype(BF16)
    lo = (x - hi.astype(F32)).astype(BF16)
    return hi, lo


def _proj_kernel(x_ref, c_ref, s_ref, ct_ref, st_ref, gpre_ref, w_ref, wqt_ref, wvt_ref, wct_ref, gq_ref,
                 wuqt_ref, wuk_ref, gkv_ref, gkvt_ref,
                 qt_ref, k_ref, kbf_ref, v_ref, vt_ref, sza_ref, szb_ref, ga_ref, gb_ref,
                 qlatt_ref, qropet_ref, ckv_ref, kcat_ref, ckvt_ref, krope_ref):
    h = _rms(x_ref[...], gpre_ref[...]).astype(BF16)

    def proj(a, b):
        return _dot(h, w_ref[:, a:b])

    qt_ref[...] = (_dot_nt(wqt_ref[...], h) * SB_SCALE).astype(BF16)
    r = proj(_C_K, _C_V)
    k_ref[...] = r
    kbf_ref[...] = r.astype(BF16)
    v_ref[...] = proj(_C_V, _C_ZA)
    vt = _dot_nt(wvt_ref[...], h).astype(BF16)
    for j in range(vt_ref.shape[0]):
        vt_ref[j] = vt[:, j * vt_ref.shape[2]:(j + 1) * vt_ref.shape[2]]
    r = proj(_C_ZA, _C_CQ)
    sza_ref[...] = (r * _sigmoid(r)).astype(BF16)
    r = proj(_C_ZB, _C_GA)
    szb_ref[...] = (r * _sigmoid(r)).astype(BF16)
    ga_ref[...] = _sigmoid(proj(_C_GA, _C_GB)).astype(BF16)
    gb_ref[...] = _sigmoid(proj(_C_GB, _C_KR)).astype(BF16)

    ckv = _rms(proj(_C_CKV, _C_ZB), gkv_ref[...])
    ckv_ref[...] = ckv
    kcat_ref[:, 0:KV_LORA] = ckv.astype(BF16)
    ct = _dot_nt(wct_ref[...], h)
    ms = jnp.mean(ct * ct, axis=0, keepdims=True)
    ctn = (ct * lax.rsqrt(ms + RMS_EPS) * gkvt_ref[...]).astype(BF16)
    for j in range(ckvt_ref.shape[0]):
        ckvt_ref[j] = ctn[:, j * ckvt_ref.shape[2]:(j + 1) * ckvt_ref.shape[2]]
    kr = proj(_C_KR, _C_KRS) * c_ref[...] + proj(_C_KRS, _N_AUG) * s_ref[...]
    krope_ref[...] = kr[:, 0:MLA_ROPE]
    kcat_ref[:, KV_LORA:KV_LORA + ROPE_TILED] = kr.astype(BF16)

    cq = _rms(proj(_C_CQ, _C_CKV), gq_ref[...]).astype(BF16)
    qmt = _dot_nt(wuqt_ref[...], cq)
    for hd in range(MLA_HEADS):
        qn = qmt[hd * NOPE_PAD:(hd + 1) * NOPE_PAD, :].astype(BF16)
        ql = _dot(wuk_ref[hd], qn) * MLA_SCALE
        qlatt_ref[hd * KV_LORA:(hd + 1) * KV_LORA, :] = ql.astype(BF16)
    qr = qmt[_R_ROPE:_R_ROPES, :] * ct_ref[...] + qmt[_R_ROPES:_N_UQ, :] * st_ref[...]
    qropet_ref[...] = (qr * MLA_SCALE).astype(BF16)


def _run_proj(x2d, cos_t, sin_t, n_table_blocks, weights, tm):
    gpre, w_aug, wqt, wvt, wct, gq, wuqt, wuk_pad, gkv, gkvt = weights
    rows = x2d.shape[0]
    nblk = rows // tm
    row = lambda t: (t, 0)
    col = lambda t: (0, t)
    tab = lambda t: (t % n_table_blocks, 0)
    tabt = lambda t: (0, t % n_table_blocks)
    const2 = lambda t: (0, 0)
    const3 = lambda t: (0, 0, 0)

    def out(width, dtype):
        return jax.ShapeDtypeStruct((rows, width), dtype), pl.BlockSpec((tm, width), row)

    def out_t(height, dtype):
        return jax.ShapeDtypeStruct((height, rows), dtype), pl.BlockSpec((height, tm), col)

    sub = max(1, tm // KEY_BLOCK)
    bw = tm // sub

    def out_blocked(height, dtype):
        return (jax.ShapeDtypeStruct((nblk * sub, height, bw), dtype),
                pl.BlockSpec((sub, height, bw), lambda t: (t, 0, 0)))

    def resident(shape, imap):
        return pl.BlockSpec(shape, imap, pipeline_mode=pl.Buffered(1))

    outs = [out_t(SB_WIDTH, BF16), out(SB_WIDTH, F32), out(SB_WIDTH, BF16), out(SB_WIDTH, F32),
            out_blocked(SB_WIDTH, BF16), out(SB_WIDTH, BF16), out(MLA_WIDTH, BF16), out(D_MODEL, BF16),
            out(D_MODEL, BF16), out_t(MLA_HEADS * KV_LORA, BF16), out_t(ROPE_TILED, BF16),
            out(KV_LORA, F32), out(KV_LORA + ROPE_TILED, BF16), out_blocked(KV_LORA, BF16),
            out(MLA_ROPE, F32)]
    cos_tt, sin_tt = cos_t.T, sin_t.T
    return pl.pallas_call(
        _proj_kernel,
        grid=(nblk,),
        in_specs=[
            pl.BlockSpec((tm, D_MODEL), row),
            pl.BlockSpec((tm, ROPE_TILED), tab),
            pl.BlockSpec((tm, ROPE_TILED), tab),
            pl.BlockSpec((ROPE_TILED, tm), tabt),
            pl.BlockSpec((ROPE_TILED, tm), tabt),
            pl.BlockSpec((1, D_MODEL), const2),
            resident(w_aug.shape, const2),
            resident(wqt.shape, const2),
            resident(wvt.shape, const2),
            resident(wct.shape, const2),
            pl.BlockSpec((1, Q_LORA), const2),
            resident(wuqt.shape, const2),
            resident(wuk_pad.shape, const3),
            pl.BlockSpec((1, KV_LORA), const2),
            pl.BlockSpec((KV_LORA, 1), const2),
        ],
        out_specs=[o[1] for o in outs],
        out_shape=[o[0] for o in outs],
        compiler_params=pltpu.CompilerParams(
            dimension_semantics=("arbitrary",), vmem_limit_bytes=VMEM_LIMIT),
        name="proj",
    )(x2d, cos_t, sin_t, cos_tt, sin_tt, gpre, w_aug, wqt, wvt, wct, gq, wuqt, wuk_pad, gkv, gkvt)


def _sb_prompt_kernel(qt_ref, k_ref, vt_ref, km_ref, vtm_ref, trit_ref, o_ref, q2_ref, carry_ref, acc_ref):
    qi = pl.program_id(1)
    t = KEY_BLOCK
    trit = trit_ref[...]
    trit2 = jnp.concatenate([trit, trit], axis=1)
    top = lax.broadcasted_iota(jnp.int32, (PAIR, t), 0) < SB_HEAD_DIM
    pairs = range(N_PAIRS)

    for p in pairs:
        qp = qt_ref[p * PAIR:(p + 1) * PAIR, :]
        zero = jnp.zeros_like(qp)
        q2_ref[p, :, 0:t] = jnp.where(top, qp, zero)
        q2_ref[p, :, t:2 * t] = jnp.where(top, zero, qp)
    carry_ref[...] = jnp.zeros(carry_ref.shape, F32)
    acc_ref[...] = jnp.zeros(acc_ref.shape, F32)

    def sweep(kb_of, vtb_of, tr2, mask):
        zs = [_dot(kb_of(p), q2_ref[p]) for p in pairs]
        sps = [_softplus(z) for z in zs]
        kept = sps if mask is None else [jnp.where(mask, sp, 0.0) for sp in sps]
        splits = [_split_bf16(sp) for sp in kept]
        locs = [_dot(tr2, jnp.concatenate([hi, lo], axis=0)) for hi, lo in splits]
        ws = [jnp.exp((z - sp) - (loc + carry_ref[p])) for p, z, sp, loc in zip(pairs, zs, sps, locs)]
        if mask is not None:
            ws = [jnp.where(mask, w, 0.0) for w in ws]
        for p in pairs:
            acc_ref[p] += _dot(vtb_of(p), ws[p].astype(BF16))
            carry_ref[p] += locs[p][0:1, :] + kept[p][0:1, :]

    def own_block(kb, mask):
        start = pl.multiple_of(kb * t, t)
        sweep(lambda p: k_ref[pl.ds(start, t), p * PAIR:(p + 1) * PAIR],
              lambda p: vt_ref[kb, p * PAIR:(p + 1) * PAIR, :], trit2, mask)

    k_in = lax.broadcasted_iota(jnp.int32, (t, 2 * t), 0)
    q_in = lax.broadcasted_iota(jnp.int32, (t, 2 * t), 1) & (t - 1)
    own_block(qi, k_in < q_in)

    def more(state):
        j, lowest = state
        return jnp.logical_and(j < qi, lowest < SB_EXIT)

    def body(state):
        j, _ = state
        own_block(qi - 1 - j, None)
        return j + 1, jnp.min(carry_ref[...])

    _, lowest = lax.while_loop(more, body, (jnp.int32(0), jnp.min(carry_ref[...])))

    @pl.when(lowest < SB_EXIT)
    def _():
        tm = km_ref.shape[0]
        m_in = lax.broadcasted_iota(jnp.int32, (tm, 2 * t), 0)
        tritm = trit_ref[0:tm, 0:tm]
        sweep(lambda p: km_ref[:, p * PAIR:(p + 1) * PAIR], lambda p: vtm_ref[p * PAIR:(p + 1) * PAIR, :],
              jnp.concatenate([tritm, tritm], axis=1), m_in < N_META)

    for p in pairs:
        acc = acc_ref[p]
        o_ref[:, p * PAIR:(p + 1) * PAIR] = jnp.where(top, acc[:, 0:t], acc[:, t:2 * t]).T


def _run_sb_prompt(qt, kbf, vt3, k_meta, vt_meta, trit, batch, seq):
    t = KEY_BLOCK
    nq = seq // t
    return pl.pallas_call(
        _sb_prompt_kernel,
        grid=(batch, nq),
        in_specs=[
            pl.BlockSpec((SB_WIDTH, t), lambda i, j: (0, i * nq + j)),
            pl.BlockSpec((seq, SB_WIDTH), lambda i, j: (i, 0)),
            pl.BlockSpec((nq, SB_WIDTH, t), lambda i, j: (i, 0, 0)),
            pl.BlockSpec(k_meta.shape, lambda i, j: (0, 0)),
            pl.BlockSpec(vt_meta.shape, lambda i, j: (0, 0)),
            pl.BlockSpec(trit.shape, lambda i, j: (0, 0)),
        ],
        out_specs=pl.BlockSpec((t, SB_WIDTH), lambda i, j: (i * nq + j, 0)),
        out_shape=jax.ShapeDtypeStruct((batch * seq, SB_WIDTH), F32),
        scratch_shapes=[
            pltpu.VMEM((N_PAIRS, PAIR, 2 * t), BF16),
            pltpu.VMEM((N_PAIRS, 1, 2 * t), F32),
            pltpu.VMEM((N_PAIRS, PAIR, 2 * t), F32),
        ],
        compiler_params=pltpu.CompilerParams(
            dimension_semantics=("arbitrary", "arbitrary"), vmem_limit_bytes=VMEM_LIMIT),
        name="sb_prompt",
    )(qt, kbf, vt3, k_meta, vt_meta, trit)


def _mla_prompt_kernel(qlatt_ref, qropet_ref, kcat_ref, ckvt_ref, kmeta_ref, ctmeta_ref, wuvt_ref,
                       o_ref, qcat_ref, m_ref, l_ref, acc_ref):
    qi = pl.program_id(1)
    tq, tk = MLA_TQ, KEY_BLOCK
    mq = MLA_HEADS * tq

    sub = lax.broadcasted_iota(jnp.int32, (ROPE_TILED, tq), 0)
    qrt = qropet_ref[...]
    for hd in range(MLA_HEADS):
        qcat_ref[0:KV_LORA, hd * tq:(hd + 1) * tq] = qlatt_ref[hd * KV_LORA:(hd + 1) * KV_LORA, :]
        own = (sub >= hd * MLA_ROPE) & (sub < (hd + 1) * MLA_ROPE)
        qcat_ref[KV_LORA:, hd * tq:(hd + 1) * tq] = jnp.where(own, qrt, jnp.zeros_like(qrt))
    qcat = qcat_ref[...]

    m_ref[...] = jnp.full(m_ref.shape, NEG_INF, F32)
    l_ref[...] = jnp.zeros(l_ref.shape, F32)
    acc_ref[...] = jnp.zeros(acc_ref.shape, F32)

    def step(kc, ct, mask):
        s = _dot(kc, qcat)
        if mask is not None:
            s = jnp.where(mask, s, NEG_INF)
        m_old = m_ref[...]
        m_new = jnp.maximum(m_old, jnp.max(s, axis=0, keepdims=True))
        alpha = jnp.exp(m_old - m_new)
        p = jnp.exp(s - m_new)
        l_ref[...] = alpha * l_ref[...] + jnp.sum(p, axis=0, keepdims=True)
        acc_ref[...] = alpha * acc_ref[...] + _dot(ct, p.astype(BF16))
        m_ref[...] = m_new

    jd = (qi * tq) // tk
    r_in = lax.broadcasted_iota(jnp.int32, (tk, mq), 0)
    c_in = lax.broadcasted_iota(jnp.int32, (tk, mq), 1) & (tq - 1)
    off = qi * tq - jd * tk
    step(kcat_ref[pl.ds(pl.multiple_of(jd * tk, tk), tk), :], ckvt_ref[jd], (r_in - c_in) <= off)

    def body(j, _):
        step(kcat_ref[pl.ds(pl.multiple_of(j * tk, tk), tk), :], ckvt_ref[j], None)
        return 0

    lax.fori_loop(0, jd, body, 0)

    tmeta = kmeta_ref.shape[0]
    mrow = lax.broadcasted_iota(jnp.int32, (tmeta, mq), 0)
    step(kmeta_ref[...], ctmeta_ref[...], mrow < N_META)

    o_lat = (acc_ref[...] / l_ref[...]).astype(BF16)
    full = _dot(wuvt_ref[...], o_lat)
    row_head = lax.broadcasted_iota(jnp.int32, (MLA_WIDTH, tq), 0) // MLA_V
    out_t = jnp.zeros((MLA_WIDTH, tq), F32)
    for hd in range(MLA_HEADS):
        out_t = out_t + jnp.where(row_head == hd, full[:, hd * tq:(hd + 1) * tq], 0.0)
    o_ref[...] = out_t.T


def _run_mla_prompt(qlatt, qropet, kcat, ckvt3, kcat_meta, ckvt_meta, wuvt, batch, seq):
    tq, tk = MLA_TQ, KEY_BLOCK
    nq = seq // tq
    nk = seq // tk
    mq = MLA_HEADS * tq
    return pl.pallas_call(
        _mla_prompt_kernel,
        grid=(batch, nq),
        in_specs=[
            pl.BlockSpec((qlatt.shape[0], tq), lambda i, j: (0, i * nq + j)),
            pl.BlockSpec((qropet.shape[0], tq), lambda i, j: (0, i * nq + j)),
            pl.BlockSpec((seq, kcat.shape[1]), lambda i, j: (i, 0)),
            pl.BlockSpec((nk, KV_LORA, tk), lambda i, j: (i, 0, 0)),
            pl.BlockSpec(kcat_meta.shape, lambda i, j: (0, 0)),
            pl.BlockSpec(ckvt_meta.shape, lambda i, j: (0, 0)),
            pl.BlockSpec(wuvt.shape, lambda i, j: (0, 0)),
        ],
        out_specs=pl.BlockSpec((tq, MLA_WIDTH), lambda i, j: (i * nq + j, 0)),
        out_shape=jax.ShapeDtypeStruct((batch * seq, MLA_WIDTH), F32),
        scratch_shapes=[
            pltpu.VMEM((KV_LORA + ROPE_TILED, mq), BF16),
            pltpu.VMEM((1, mq), F32),
            pltpu.VMEM((1, mq), F32),
            pltpu.VMEM((KV_LORA, mq), F32),
        ],
        compiler_params=pltpu.CompilerParams(
            dimension_semantics=("arbitrary", "arbitrary"), vmem_limit_bytes=VMEM_LIMIT),
        name="mla_prompt",
    )(qlatt, qropet, kcat, ckvt3, kcat_meta, ckvt_meta, wuvt)


def _rev_cumsum_blocks(lk, tri):
    rows, width = lk.shape
    cb = tri.shape[0]
    nb = width // cb
    blocks = [lk[:, b * cb:(b + 1) * cb] for b in range(nb)]
    stacked = blocks[0] if nb == 1 else jnp.concatenate(blocks, axis=0)
    hi, lo = _split_bf16(stacked)
    loc = _dot(hi, tri) + _dot(lo, tri)
    tots = jnp.sum(stacked, axis=-1, keepdims=True)
    outs = [None] * nb
    off = jnp.zeros((rows, 1), F32)
    for b in range(nb - 1, -1, -1):
        outs[b] = loc[b * rows:(b + 1) * rows, :] + off
        off = off + tots[b * rows:(b + 1) * rows, :]
    later = outs[0] if nb == 1 else jnp.concatenate(outs, axis=1)
    return later, off


def _sb_weights(z, tri, mask, carry):
    sp = _softplus(z)
    lk = -sp if mask is None else jnp.where(mask, -sp, 0.0)
    later, total = _rev_cumsum_blocks(lk, tri)
    w = jnp.exp((z - sp) + (later + carry))
    if mask is not None:
        w = jnp.where(mask, w, 0.0)
    return w.astype(BF16), carry + total


def _softmax_update(s, mask, m, l):
    if mask is not None:
        s = jnp.where(mask, s, NEG_INF)
    m_new = jnp.maximum(m, jnp.max(s, axis=-1, keepdims=True))
    alpha = jnp.exp(m - m_new)
    p = jnp.exp(s - m_new)
    return p.astype(BF16), alpha, m_new, alpha * l + jnp.sum(p, axis=-1, keepdims=True)


def _head_diag_rows(full):
    rows, width = full.shape
    nq = rows // SB_HEADS
    row_head = lax.broadcasted_iota(jnp.int32, (rows, width), 0) // nq
    col_head = lax.broadcasted_iota(jnp.int32, (rows, width), 1) // SB_HEAD_DIM
    kept = jnp.where(row_head == col_head, full, 0.0)
    out = kept[0:nq, :]
    for hd in range(1, SB_HEADS):
        out = out + kept[hd * nq:(hd + 1) * nq, :]
    return out


def _decode_kernel(pt_ref, qbd_ref, qlat_ref, qrope_ref, knew_ref, vnew_ref, cnew_ref, rnew_ref,
                   tri_ref, wuv_ref, ck_hbm, cv_hbm, cc_hbm, cr_hbm, oa_ref, ob_ref,
                   kbuf, vbuf, cbuf, rbuf, acct_ref, sem, sem_sb):
    s = pl.program_id(0)
    n_seq = pl.num_programs(0)
    n_pages = pt_ref.shape[1]
    g = PAGES_PER_CHUNK
    n_chunks = n_pages // g
    page = cbuf.shape[1] // g
    gs = SB_PAGES_PER_CHUNK
    n_sb_chunks = n_pages // gs

    def copies(seq, chunk, slot):
        out = []
        for p in range(g):
            pg = pt_ref[seq, chunk * g + p]
            span = pl.ds(p * page, page)
            out.append(pltpu.make_async_copy(cc_hbm.at[pg], cbuf.at[slot, span, :], sem.at[slot]))
            out.append(pltpu.make_async_copy(cr_hbm.at[pg], rbuf.at[slot, :, span], sem.at[slot]))
        return out

    def sb_copies(seq, k, slot):
        out = []
        for p in range(gs):
            pg = pt_ref[seq, n_pages - (k + 1) * gs + p]
            span = pl.ds(p * page, page)
            out.append(pltpu.make_async_copy(ck_hbm.at[pg], kbuf.at[slot, :, span], sem_sb.at[slot]))
            out.append(pltpu.make_async_copy(cv_hbm.at[pg], vbuf.at[slot, :, span], sem_sb.at[slot]))
        return out

    def start_all(cs):
        for c in cs:
            c.start()

    def wait_all(cs):
        for c in cs:
            c.wait()

    def start_global(gidx):
        seq = lax.div(gidx, n_chunks)
        start_all(copies(seq, n_chunks - 1 - lax.rem(gidx, n_chunks), lax.rem(gidx, N_SLOTS)))

    @pl.when(s == 0)
    def _():
        start_all(sb_copies(0, 0, 0))
        for ahead in range(N_SLOTS - 1):
            start_all(copies(ahead // n_chunks, n_chunks - 1 - ahead % n_chunks, ahead % N_SLOTS))

    @pl.when(s + 1 < n_seq)
    def _():
        start_all(sb_copies(s + 1, 0, (s + 1) & 1))

    qbd = qbd_ref[...]
    qlat = qlat_ref[...]
    qrope = qrope_ref[...]
    tri = tri_ref[...]
    rows = qbd.shape[0]
    nq = rows // SB_HEADS

    n_new = knew_ref.shape[0]

    def pad_new(ref):
        x = ref[...]
        return jnp.concatenate([x, jnp.zeros((page - n_new, x.shape[1]), x.dtype)], axis=0).astype(BF16)

    q_idx = lax.broadcasted_iota(jnp.int32, (rows, page), 0) & (nq - 1)
    k_idx = lax.broadcasted_iota(jnp.int32, (rows, page), 1)
    w, carry = _sb_weights(_dot_nt(qbd, pad_new(knew_ref)), tri_ref[0:page, 0:page], k_idx < q_idx,
                           jnp.zeros((rows, 1), F32))
    oa_new = _dot(w, pad_new(vnew_ref))
    cnew = pad_new(cnew_ref)
    p, _, m, l = _softmax_update(_dot_nt(qlat, cnew) + _dot_nt(qrope, pad_new(rnew_ref)),
                                 k_idx <= q_idx, jnp.full((rows, 1), NEG_INF, F32),
                                 jnp.zeros((rows, 1), F32))
    acc_b = _dot(p, cnew)
    acct_ref[...] = jnp.zeros(acct_ref.shape, F32)

    def sb_chunk(slot, carry):
        w, carry = _sb_weights(_dot(qbd, kbuf[slot].astype(BF16)), tri, None, carry)
        acct_ref[:, 0:rows] += _dot_nt(vbuf[slot].astype(BF16), w)
        return carry

    first = s & 1
    wait_all(sb_copies(s, 0, first))
    carry = sb_chunk(first, carry)

    def sb_more(state):
        k, carry = state
        return jnp.logical_and(k < n_sb_chunks, jnp.max(carry) > -SB_EXIT)

    def sb_body(state):
        k, carry = state
        start_all(sb_copies(s, k, 2))
        wait_all(sb_copies(s, k, 2))
        return k + 1, sb_chunk(2, carry)

    lax.while_loop(sb_more, sb_body, (jnp.int32(1), carry))

    def body(k, state):
        m, l, acc_b = state
        chunk = n_chunks - 1 - k
        gidx = s * n_chunks + k
        slot = lax.rem(gidx, N_SLOTS)

        @pl.when(gidx + (N_SLOTS - 1) < n_seq * n_chunks)
        def _():
            start_global(gidx + (N_SLOTS - 1))

        wait_all(copies(s, chunk, slot))
        cc = cbuf[slot].astype(BF16)
        p, alpha, m, l = _softmax_update(_dot_nt(qlat, cc) + _dot(qrope, rbuf[slot].astype(BF16)),
                                         None, m, l)
        acc_b = alpha * acc_b + _dot(p, cc)
        return m, l, acc_b

    m, l, acc_b = lax.fori_loop(0, n_chunks, body, (m, l, acc_b))

    oa_ref[...] = _head_diag_rows(acct_ref[...].T[0:rows, :] + oa_new)
    o_lat = (acc_b / l).astype(BF16)
    ob_ref[...] = _head_diag_rows(_dot(o_lat, wuv_ref[...]))


def _run_decode(page_table, qbd, qlat, qrope, knew, vnew, cnew, rnew, tri, wuv_all,
                cache_kt, cache_vt, cache_c, cache_rt):
    n_seq, n_pages = page_table.shape
    page = cache_c.shape[1]
    rows = qbd.shape[1]
    nq = rows // SB_HEADS
    n_new = knew.shape[1]
    tok = PAGES_PER_CHUNK * page
    sb_tok = SB_PAGES_PER_CHUNK * page
    per_seq = lambda i, pt: (i, 0, 0)
    const2 = lambda i, pt: (0, 0)
    grid_spec = pltpu.PrefetchScalarGridSpec(
        num_scalar_prefetch=1,
        grid=(n_seq,),
        in_specs=[
            pl.BlockSpec((None, rows, SB_WIDTH), per_seq),
            pl.BlockSpec((None, rows, KV_LORA), per_seq),
            pl.BlockSpec((None, rows, MLA_ROPE), per_seq),
            pl.BlockSpec((None, n_new, SB_WIDTH), per_seq),
            pl.BlockSpec((None, n_new, SB_WIDTH), per_seq),
            pl.BlockSpec((None, n_new, KV_LORA), per_seq),
            pl.BlockSpec((None, n_new, MLA_ROPE), per_seq),
            pl.BlockSpec(tri.shape, const2),
            pl.BlockSpec(wuv_all.shape, const2),
            pl.BlockSpec(memory_space=pl.ANY),
            pl.BlockSpec(memory_space=pl.ANY),
            pl.BlockSpec(memory_space=pl.ANY),
            pl.BlockSpec(memory_space=pl.ANY),
        ],
        out_specs=[
            pl.BlockSpec((None, nq, SB_WIDTH), per_seq),
            pl.BlockSpec((None, nq, MLA_WIDTH), per_seq),
        ],
        scratch_shapes=[
            pltpu.VMEM((3, SB_WIDTH, sb_tok), F32),
            pltpu.VMEM((3, SB_WIDTH, sb_tok), F32),
            pltpu.VMEM((N_SLOTS, tok, KV_LORA), F32),
            pltpu.VMEM((N_SLOTS, MLA_ROPE, tok), F32),
            pltpu.VMEM((SB_WIDTH, LANES), F32),
            pltpu.SemaphoreType.DMA((N_SLOTS,)),
            pltpu.SemaphoreType.DMA((3,)),
        ],
    )
    return pl.pallas_call(
        _decode_kernel,
        grid_spec=grid_spec,
        out_shape=[jax.ShapeDtypeStruct((n_seq, nq, SB_WIDTH), F32),
                   jax.ShapeDtypeStruct((n_seq, nq, MLA_WIDTH), F32)],
        compiler_params=pltpu.CompilerParams(
            dimension_semantics=("arbitrary",), vmem_limit_bytes=VMEM_LIMIT),
        name="decode",
    )(page_table, qbd, qlat, qrope, knew, vnew, cnew, rnew, tri, wuv_all,
      cache_kt, cache_vt, cache_c, cache_rt)


def _out_kernel(x_ref, oa_ref, ob_ref, sza_ref, szb_ref, ga_ref, gb_ref, wpa_ref, wpb_ref,
                wout_ref, gpost_ref, y_ref):
    a = (oa_ref[...] * sza_ref[...]).astype(BF16)
    b = (ob_ref[...] * szb_ref[...]).astype(BF16)
    merged = ga_ref[...] * _dot(a, wpa_ref[...]) + gb_ref[...] * _dot(b, wpb_ref[...])
    t = _dot(merged.astype(BF16), wout_ref[...])
    y_ref[...] = x_ref[...] + _rms(t, gpost_ref[...])


def _run_out(x2d, oa, ob, sza, szb, ga, gb, wpa, wpb, wout, gpost, tm):
    rows = x2d.shape[0]
    row = lambda t: (t, 0)
    const2 = lambda t: (0, 0)
    return pl.pallas_call(
        _out_kernel,
        grid=(rows // tm,),
        in_specs=[
            pl.BlockSpec((tm, D_MODEL), row),
            pl.BlockSpec((tm, SB_WIDTH), row),
            pl.BlockSpec((tm, MLA_WIDTH), row),
            pl.BlockSpec((tm, SB_WIDTH), row),
            pl.BlockSpec((tm, MLA_WIDTH), row),
            pl.BlockSpec((tm, D_MODEL), row),
            pl.BlockSpec((tm, D_MODEL), row),
            pl.BlockSpec(wpa.shape, const2),
            pl.BlockSpec(wpb.shape, const2),
            pl.BlockSpec(wout.shape, const2),
            pl.BlockSpec((1, D_MODEL), const2),
        ],
        out_specs=pl.BlockSpec((tm, D_MODEL), row),
        out_shape=jax.ShapeDtypeStruct((rows, D_MODEL), F32),
        compiler_params=pltpu.CompilerParams(
            dimension_semantics=("arbitrary",), vmem_limit_bytes=VMEM_LIMIT),
        name="out_mix",
    )(x2d, oa, ob, sza, szb, ga, gb, wpa, wpb, wout, gpost)


def _rope_tables(pos):
    half = MLA_ROPE // 2
    inv = ROPE_THETA ** (-jnp.arange(half, dtype=jnp.float32) * 2.0 / MLA_ROPE)
    ang = pos.astype(jnp.float32)[:, None] * inv[None, :]
    cos, sin = jnp.cos(ang), jnp.sin(ang)
    cos_t = jnp.repeat(cos, 2, axis=1)
    sin_t = jnp.stack([-sin, sin], axis=-1).reshape(pos.shape[0], MLA_ROPE)
    return jnp.tile(cos_t, (1, MLA_HEADS)), jnp.tile(sin_t, (1, MLA_HEADS))


def kernel(x_prompt, x_sample, cache_sb_k, cache_sb_v, cache_mla_ckv, cache_mla_krope, page_table,
           meta_tokens, g_pre, w_in, g_qnorm, w_uq, g_kvnorm, w_uk, w_uv, w_proj_a, w_proj_b,
           w_out, g_post):
    assert g_pre.shape[0] == 1, "single layer step"
    batch, seq, d_model = x_prompt.shape
    n_dec, dec_seq, _ = x_sample.shape
    n_pool, page = cache_sb_k.shape[1], cache_sb_k.shape[2]
    past_len = page_table.shape[1] * page
    pair_swap = jnp.arange(MLA_ROPE) ^ 1

    w = w_in[0]
    o_ckv = 4 * SB_WIDTH + Q_LORA
    o_kr = o_ckv + KV_LORA
    w_kr = w[:, o_kr:o_kr + MLA_ROPE]
    w_aug = jnp.concatenate(
        [w[:, :o_kr], w[:, o_kr + MLA_ROPE:], jnp.tile(w_kr, (1, MLA_HEADS)),
         jnp.tile(w_kr[:, pair_swap], (1, MLA_HEADS))], axis=1).astype(BF16)
    assert w_aug.shape[1] == _N_AUG
    wqt = w[:, :SB_WIDTH].T.astype(BF16)
    wvt = w[:, 2 * SB_WIDTH:3 * SB_WIDTH].T.astype(BF16)
    wct = w[:, o_ckv:o_ckv + KV_LORA].T.astype(BF16)
    wq = w_uq[0].reshape(Q_LORA, MLA_HEADS, MLA_NOPE + MLA_ROPE)
    wq_nope = jnp.pad(wq[:, :, :MLA_NOPE], ((0, 0), (0, 0), (0, NOPE_PAD - MLA_NOPE)))
    wq_rope = wq[:, :, MLA_NOPE:]
    wuqt = jnp.concatenate(
        [wq_nope.reshape(Q_LORA, -1), wq_rope.reshape(Q_LORA, -1),
         wq_rope[:, :, pair_swap].reshape(Q_LORA, -1)], axis=1).T.astype(BF16)
    assert wuqt.shape[0] == _N_UQ
    wuk_pad = jnp.pad(w_uk[0], ((0, 0), (0, 0), (0, NOPE_PAD - MLA_NOPE))).astype(BF16)
    wuv_all = jnp.swapaxes(w_uv[0], 0, 1).reshape(KV_LORA, MLA_WIDTH).astype(BF16)
    wuvt = wuv_all.T
    wpa = w_proj_a[0].astype(BF16)
    wpb = w_proj_b[0].astype(BF16)
    wout = w_out[0].astype(BF16)
    idx = jnp.arange(KEY_BLOCK)
    tri = (idx[:, None] > idx[None, :]).astype(BF16)
    weights = (g_pre, w_aug, wqt, wvt, wct, g_qnorm, wuqt, wuk_pad, g_kvnorm, g_kvnorm.reshape(KV_LORA, 1))

    xp2 = x_prompt.reshape(batch * seq, d_model)
    cos_p, sin_p = _rope_tables(N_META + jnp.arange(seq, dtype=jnp.int32))
    proj_p = _run_proj(xp2, cos_p, sin_p, seq // PROJ_TM, weights, PROJ_TM)
    n_s = n_dec * dec_seq
    n_small = -(-(n_s + N_META) // SMALL_TM) * SMALL_TM
    xs2 = x_sample.reshape(n_s, d_model)
    x_small = jnp.concatenate([xs2, meta_tokens.astype(x_prompt.dtype),
                               jnp.zeros((n_small - n_s - N_META, d_model), x_prompt.dtype)], axis=0)
    pos_small = jnp.concatenate([past_len + (jnp.arange(n_s, dtype=jnp.int32) % dec_seq),
                                 jnp.arange(n_small - n_s, dtype=jnp.int32)])
    cos_s, sin_s = _rope_tables(pos_small)
    proj_s = _run_proj(x_small, cos_s, sin_s, n_small // SMALL_TM, weights, SMALL_TM)

    (qt_p, k_p, kbf_p, v_p, vt_p, sza_p, szb_p, ga_p, gb_p, qlatt_p, qropet_p, ckv_p, kcat_p,
     ckvt_p, krope_p) = proj_p
    (qt_s, k_s, kbf_s, v_s, vt_s, sza_s, szb_s, ga_s, gb_s, qlatt_s, qropet_s, ckv_s, kcat_s,
     ckvt_s, krope_s) = proj_s
    assert n_s % SMALL_TM == 0
    meta_blk = n_s // SMALL_TM

    oa_p = _run_sb_prompt(qt_p, kbf_p, vt_p, kbf_s[n_s:n_s + SMALL_TM], vt_s[meta_blk], tri.T,
                          batch, seq)
    ob_p = _run_mla_prompt(qlatt_p, qropet_p, kcat_p, ckvt_p, kcat_s[n_s:n_s + SMALL_TM],
                           ckvt_s[meta_blk], wuvt, batch, seq)
    y_p = _run_out(xp2, oa_p, ob_p, sza_p, szb_p, ga_p, gb_p, wpa, wpb, wout, g_post, OUT_TM)

    rows = SB_HEADS * dec_seq
    q4 = qt_s[:, :n_s].T.reshape(n_dec, dec_seq, SB_WIDTH)
    qbd = jnp.tile(q4, (1, SB_HEADS, 1)).reshape(n_dec, SB_HEADS, dec_seq, SB_WIDTH)
    own = (jnp.arange(SB_WIDTH)[None, :] // SB_HEAD_DIM) == jnp.arange(SB_HEADS)[:, None]
    qbd = jnp.where(own[None, :, None, :], qbd, jnp.zeros_like(qbd)).reshape(n_dec, rows, SB_WIDTH)
    qlat_d = qlatt_s[:, :n_s].reshape(MLA_HEADS, KV_LORA, n_dec, dec_seq).transpose(2, 0, 3, 1)
    qlat_d = qlat_d.reshape(n_dec, rows, KV_LORA)
    qrope_d = qropet_s[:, :n_s].reshape(MLA_HEADS, MLA_ROPE, n_dec, dec_seq).transpose(2, 0, 3, 1)
    qrope_d = qrope_d.reshape(n_dec, rows, MLA_ROPE)

    def new_rows(a):
        a = a[:n_s].reshape(n_dec, dec_seq, a.shape[-1])
        return jnp.pad(a, ((0, 0), (0, 8 - dec_seq), (0, 0)))

    cache_kt = cache_sb_k[0].transpose(0, 2, 3, 1).reshape(n_pool, SB_WIDTH, page)
    cache_vt = cache_sb_v[0].transpose(0, 2, 3, 1).reshape(n_pool, SB_WIDTH, page)
    cache_rt = cache_mla_krope[0].transpose(0, 2, 1)
    oa_s, ob_s = _run_decode(
        page_table, qbd, qlat_d, qrope_d, new_rows(k_s), new_rows(v_s), new_rows(ckv_s),
        new_rows(krope_s), tri, wuv_all, cache_kt, cache_vt, cache_mla_ckv[0], cache_rt)
    y_s = _run_out(xs2, oa_s.reshape(n_s, SB_WIDTH), ob_s.reshape(n_s, MLA_WIDTH), sza_s[:n_s],
                   szb_s[:n_s], ga_s[:n_s], gb_s[:n_s], wpa, wpb, wout, g_post, n_s)

    def with_meta(small, big):
        width = big.shape[-1]
        meta = jnp.broadcast_to(small[n_s:n_s + N_META][None], (batch, N_META, width))
        return jnp.concatenate([meta, big.reshape(batch, seq, width)], axis=1)[None]

    lp = seq + N_META
    return (
        y_p.reshape(batch, seq, d_model),
        y_s.reshape(n_dec, dec_seq, d_model),
        with_meta(k_s, k_p).reshape(1, batch, lp, SB_HEADS, SB_HEAD_DIM),
        with_meta(v_s, v_p).reshape(1, batch, lp, SB_HEADS, SB_HEAD_DIM),
        with_meta(ckv_s, ckv_p),
        with_meta(krope_s, krope_p),
        k_s[:n_s].reshape(1, n_dec, dec_seq, SB_HEADS, SB_HEAD_DIM),
        v_s[:n_s].reshape(1, n_dec, dec_seq, SB_HEADS, SB_HEAD_DIM),
        ckv_s[:n_s].reshape(1, n_dec, dec_seq, KV_LORA),
        krope_s[:n_s].reshape(1, n_dec, dec_seq, MLA_ROPE),
    )
```

```python
import jax
import jax.numpy as jnp
from jax import lax
from jax.experimental import pallas as pl
from jax.experimental.pallas import tpu as pltpu

F32 = jnp.float32
BF16 = jnp.bfloat16

N_META = 16
SB_HEADS = 8
SB_HEAD_DIM = 64
SB_WIDTH = SB_HEADS * SB_HEAD_DIM
MLA_HEADS = 8
MLA_NOPE = 64
MLA_ROPE = 32
MLA_V = 64
MLA_WIDTH = MLA_HEADS * MLA_V
Q_LORA = 384
KV_LORA = 256
D_MODEL = 1024
MLA_SCALE = (MLA_NOPE + MLA_ROPE) ** -0.5
SB_SCALE = SB_HEAD_DIM ** -0.5
ROPE_THETA = 10000.0
RMS_EPS = 1e-6
NEG_INF = -1e30
SB_EXIT = 120.0

LANES = 128
ROPE_TILED = MLA_HEADS * MLA_ROPE
NOPE_PAD = LANES
PAIR = 2 * SB_HEAD_DIM
N_PAIRS = SB_WIDTH // PAIR

_C_Q, _C_K, _C_V, _C_ZA = 0, 512, 1024, 1536
_C_CQ = 2048
_C_CKV = _C_CQ + Q_LORA
_C_ZB = _C_CKV + KV_LORA
_C_GA = _C_ZB + MLA_WIDTH
_C_GB = _C_GA + D_MODEL
_C_KR = _C_GB + D_MODEL
_C_KRS = _C_KR + ROPE_TILED
_N_AUG = _C_KRS + ROPE_TILED
_R_ROPE = MLA_HEADS * NOPE_PAD
_R_ROPES = _R_ROPE + ROPE_TILED
_N_UQ = _R_ROPES + ROPE_TILED

VMEM_LIMIT = 56 * 1024 * 1024

KEY_BLOCK = 256
PROJ_TM = 512
SMALL_TM = 128
OUT_TM = 512
MLA_TQ = 256
PAGES_PER_CHUNK = 32
N_SLOTS = 3
SB_PAGES_PER_CHUNK = 2


def _dot(a, b):
    return jnp.dot(a, b, preferred_element_type=F32)


def _dot_nt(a, b):
    return lax.dot_general(a, b, (((1,), (1,)), ((), ())), preferred_element_type=F32)


def _rms(x, g):
    ms = jnp.mean(x * x, axis=-1, keepdims=True)
    return x * lax.rsqrt(ms + RMS_EPS) * g


def _softplus(z):
    return jnp.maximum(z, 0.0) + jnp.log(1.0 + jnp.exp(-jnp.abs(z)))


def _sigmoid(z):
    return 1.0 / (1.0 + jnp.exp(-z))


def _split_bf16(x):
    hi = x.astype(BF16)
    lo = (x - hi.astype(F32)).astype(BF16)
    return hi, lo


def _proj_kernel(x_ref, c_ref, s_ref, ct_ref, st_ref, gpre_ref, w_ref, wqt_ref, wvt_ref, wct_ref, gq_ref,
                 wuqt_ref, wuk_ref, gkv_ref, gkvt_ref,
                 qt_ref, k_ref, kbf_ref, v_ref, vt_ref, sza_ref, szb_ref, ga_ref, gb_ref,
                 qlatt_ref, qropet_ref, ckv_ref, kcat_ref, ckvt_ref, krope_ref):
    h = _rms(x_ref[...], gpre_ref[...]).astype(BF16)

    def proj(a, b):
        return _dot(h, w_ref[:, a:b])

    qt_ref[...] = (_dot_nt(wqt_ref[...], h) * SB_SCALE).astype(BF16)
    r = proj(_C_K, _C_V)
    k_ref[...] = r
    kbf_ref[...] = r.astype(BF16)
    v_ref[...] = proj(_C_V, _C_ZA)
    vt = _dot_nt(wvt_ref[...], h).astype(BF16)
    for j in range(vt_ref.shape[0]):
        vt_ref[j] = vt[:, j * vt_ref.shape[2]:(j + 1) * vt_ref.shape[2]]
    r = proj(_C_ZA, _C_CQ)
    sza_ref[...] = (r * _sigmoid(r)).astype(BF16)
    r = proj(_C_ZB, _C_GA)
    szb_ref[...] = (r * _sigmoid(r)).astype(BF16)
    ga_ref[...] = _sigmoid(proj(_C_GA, _C_GB)).astype(BF16)
    gb_ref[...] = _sigmoid(proj(_C_GB, _C_KR)).astype(BF16)

    ckv = _rms(proj(_C_CKV, _C_ZB), gkv_ref[...])
    ckv_ref[...] = ckv
    kcat_ref[:, 0:KV_LORA] = ckv.astype(BF16)
    ct = _dot_nt(wct_ref[...], h)
    ms = jnp.mean(ct * ct, axis=0, keepdims=True)
    ctn = (ct * lax.rsqrt(ms + RMS_EPS) * gkvt_ref[...]).astype(BF16)
    for j in range(ckvt_ref.shape[0]):
        ckvt_ref[j] = ctn[:, j * ckvt_ref.shape[2]:(j + 1) * ckvt_ref.shape[2]]
    kr = proj(_C_KR, _C_KRS) * c_ref[...] + proj(_C_KRS, _N_AUG) * s_ref[...]
    krope_ref[...] = kr[:, 0:MLA_ROPE]
    kcat_ref[:, KV_LORA:KV_LORA + ROPE_TILED] = kr.astype(BF16)

    cq = _rms(proj(_C_CQ, _C_CKV), gq_ref[...]).astype(BF16)
    qmt = _dot_nt(wuqt_ref[...], cq)
    for hd in range(MLA_HEADS):
        qn = qmt[hd * NOPE_PAD:(hd + 1) * NOPE_PAD, :].astype(BF16)
        ql = _dot(wuk_ref[hd], qn) * MLA_SCALE
        qlatt_ref[hd * KV_LORA:(hd + 1) * KV_LORA, :] = ql.astype(BF16)
    qr = qmt[_R_ROPE:_R_ROPES, :] * ct_ref[...] + qmt[_R_ROPES:_N_UQ, :] * st_ref[...]
    qropet_ref[...] = (qr * MLA_SCALE).astype(BF16)


N_PROJ_IN = 15
PROJ_CACHE_OUTS = (1, 3, 11, 14)


def _proj_kernel_into(*refs):
    _proj_kernel(*refs[:N_PROJ_IN], *refs[N_PROJ_IN + len(PROJ_CACHE_OUTS):])


def _run_proj(x2d, cos_t, sin_t, weights, tm, batch=1, lead=0, bases=None):
    gpre, w_aug, wqt, wvt, wct, gq, wuqt, wuk_pad, gkv, gkvt = weights
    rows = x2d.shape[0]
    nblk = rows // tm
    per = nblk // batch
    assert cos_t.shape[0] == per * tm
    slab = lead + per * tm
    row = lambda b, j: (b * per + j, 0)
    col = lambda b, j: (0, b * per + j)
    tab = lambda b, j: (j, 0)
    tabt = lambda b, j: (0, j)
    const2 = lambda b, j: (0, 0)
    const3 = lambda b, j: (0, 0, 0)

    def out(width, dtype):
        return jax.ShapeDtypeStruct((rows, width), dtype), pl.BlockSpec((tm, width), row)

    def out_cache(width, dtype):
        if bases is None:
            return out(width, dtype)
        return (jax.ShapeDtypeStruct((batch * slab, width), dtype),
                pl.BlockSpec((pl.Element(tm), pl.Element(width)),
                             lambda b, j: (pl.multiple_of(b * slab + lead + j * tm, 8), 0)))

    def out_t(height, dtype):
        return jax.ShapeDtypeStruct((height, rows), dtype), pl.BlockSpec((height, tm), col)

    sub = max(1, tm // KEY_BLOCK)
    bw = tm // sub

    def out_blocked(height, dtype):
        return (jax.ShapeDtypeStruct((nblk * sub, height, bw), dtype),
                pl.BlockSpec((sub, height, bw), lambda b, j: (b * per + j, 0, 0)))

    def resident(shape, imap):
        return pl.BlockSpec(shape, imap, pipeline_mode=pl.Buffered(1))

    outs = [out_t(SB_WIDTH, BF16), out_cache(SB_WIDTH, F32), out(SB_WIDTH, BF16), out_cache(SB_WIDTH, F32),
            out_blocked(SB_WIDTH, BF16), out(SB_WIDTH, BF16), out(MLA_WIDTH, BF16), out(D_MODEL, BF16),
            out(D_MODEL, BF16), out_t(MLA_HEADS * KV_LORA, BF16), out_t(ROPE_TILED, BF16),
            out_cache(KV_LORA, F32), out(KV_LORA + ROPE_TILED, BF16), out_blocked(KV_LORA, BF16),
            out_cache(MLA_ROPE, F32)]
    cos_tt, sin_tt = cos_t.T, sin_t.T
    donated = () if bases is None else tuple(bases)
    return pl.pallas_call(
        _proj_kernel if bases is None else _proj_kernel_into,
        grid=(batch, per),
        input_output_aliases={N_PROJ_IN + i: o for i, o in enumerate(PROJ_CACHE_OUTS)} if donated else {},
        in_specs=[
            pl.BlockSpec((tm, D_MODEL), row),
            pl.BlockSpec((tm, ROPE_TILED), tab),
            pl.BlockSpec((tm, ROPE_TILED), tab),
            pl.BlockSpec((ROPE_TILED, tm), tabt),
            pl.BlockSpec((ROPE_TILED, tm), tabt),
            pl.BlockSpec((1, D_MODEL), const2),
            resident(w_aug.shape, const2),
            resident(wqt.shape, const2),
            resident(wvt.shape, const2),
            resident(wct.shape, const2),
            pl.BlockSpec((1, Q_LORA), const2),
            resident(wuqt.shape, const2),
            resident(wuk_pad.shape, const3),
            pl.BlockSpec((1, KV_LORA), const2),
            pl.BlockSpec((KV_LORA, 1), const2),
        ] + [pl.BlockSpec(memory_space=pl.ANY)] * len(donated),
        out_specs=[o[1] for o in outs],
        out_shape=[o[0] for o in outs],
        compiler_params=pltpu.CompilerParams(
            dimension_semantics=("arbitrary", "arbitrary"), vmem_limit_bytes=VMEM_LIMIT),
        name="proj",
    )(x2d, cos_t, sin_t, cos_tt, sin_tt, gpre, w_aug, wqt, wvt, wct, gq, wuqt, wuk_pad, gkv, gkvt,
      *donated)


def _sb_prompt_kernel(qt_ref, k_ref, vt_ref, km_ref, vtm_ref, trit_ref, o_ref, q2_ref, carry_ref, acc_ref):
    qi = pl.program_id(1)
    t = KEY_BLOCK
    trit = trit_ref[...]
    trit2 = jnp.concatenate([trit, trit], axis=1)
    top = lax.broadcasted_iota(jnp.int32, (PAIR, t), 0) < SB_HEAD_DIM
    pairs = range(N_PAIRS)

    for p in pairs:
        qp = qt_ref[p * PAIR:(p + 1) * PAIR, :]
        zero = jnp.zeros_like(qp)
        q2_ref[p, :, 0:t] = jnp.where(top, qp, zero)
        q2_ref[p, :, t:2 * t] = jnp.where(top, zero, qp)
    carry_ref[...] = jnp.zeros(carry_ref.shape, F32)
    acc_ref[...] = jnp.zeros(acc_ref.shape, F32)

    def sweep(kb_of, vtb_of, tr2, mask):
        zs = [_dot(kb_of(p), q2_ref[p]) for p in pairs]
        sps = [_softplus(z) for z in zs]
        kept = sps if mask is None else [jnp.where(mask, sp, 0.0) for sp in sps]
        splits = [_split_bf16(sp) for sp in kept]
        locs = [_dot(tr2, jnp.concatenate([hi, lo], axis=0)) for hi, lo in splits]
        ws = [jnp.exp((z - sp) - (loc + carry_ref[p])) for p, z, sp, loc in zip(pairs, zs, sps, locs)]
        if mask is not None:
            ws = [jnp.where(mask, w, 0.0) for w in ws]
        for p in pairs:
            acc_ref[p] += _dot(vtb_of(p), ws[p].astype(BF16))
            carry_ref[p] += locs[p][0:1, :] + kept[p][0:1, :]

    def own_block(kb, mask):
        start = pl.multiple_of(kb * t, t)
        sweep(lambda p: k_ref[pl.ds(start, t), p * PAIR:(p + 1) * PAIR],
              lambda p: vt_ref[kb, p * PAIR:(p + 1) * PAIR, :], trit2, mask)

    k_in = lax.broadcasted_iota(jnp.int32, (t, 2 * t), 0)
    q_in = lax.broadcasted_iota(jnp.int32, (t, 2 * t), 1) & (t - 1)
    own_block(qi, k_in < q_in)

    def more(state):
        j, lowest = state
        return jnp.logical_and(j < qi, lowest < SB_EXIT)

    def body(state):
        j, _ = state
        own_block(qi - 1 - j, None)
        return j + 1, jnp.min(carry_ref[...])

    _, lowest = lax.while_loop(more, body, (jnp.int32(0), jnp.min(carry_ref[...])))

    @pl.when(lowest < SB_EXIT)
    def _():
        tm = km_ref.shape[0]
        m_in = lax.broadcasted_iota(jnp.int32, (tm, 2 * t), 0)
        tritm = trit_ref[0:tm, 0:tm]
        sweep(lambda p: km_ref[:, p * PAIR:(p + 1) * PAIR], lambda p: vtm_ref[p * PAIR:(p + 1) * PAIR, :],
              jnp.concatenate([tritm, tritm], axis=1), m_in < N_META)

    for p in pairs:
        acc = acc_ref[p]
        o_ref[:, p * PAIR:(p + 1) * PAIR] = jnp.where(top, acc[:, 0:t], acc[:, t:2 * t]).T


def _run_sb_prompt(qt, kbf, vt3, k_meta, vt_meta, trit, batch, seq):
    t = KEY_BLOCK
    nq = seq // t
    return pl.pallas_call(
        _sb_prompt_kernel,
        grid=(batch, nq),
        in_specs=[
            pl.BlockSpec((SB_WIDTH, t), lambda i, j: (0, i * nq + j)),
            pl.BlockSpec((seq, SB_WIDTH), lambda i, j: (i, 0)),
            pl.BlockSpec((nq, SB_WIDTH, t), lambda i, j: (i, 0, 0)),
            pl.BlockSpec(k_meta.shape, lambda i, j: (0, 0)),
            pl.BlockSpec(vt_meta.shape, lambda i, j: (0, 0)),
            pl.BlockSpec(trit.shape, lambda i, j: (0, 0)),
        ],
        out_specs=pl.BlockSpec((t, SB_WIDTH), lambda i, j: (i * nq + j, 0)),
        out_shape=jax.ShapeDtypeStruct((batch * seq, SB_WIDTH), F32),
        scratch_shapes=[
            pltpu.VMEM((N_PAIRS, PAIR, 2 * t), BF16),
            pltpu.VMEM((N_PAIRS, 1, 2 * t), F32),
            pltpu.VMEM((N_PAIRS, PAIR, 2 * t), F32),
        ],
        compiler_params=pltpu.CompilerParams(
            dimension_semantics=("arbitrary", "arbitrary"), vmem_limit_bytes=VMEM_LIMIT),
        name="sb_prompt",
    )(qt, kbf, vt3, k_meta, vt_meta, trit)


def _mla_prompt_kernel(qlatt_ref, qropet_ref, kcat_ref, ckvt_ref, kmeta_ref, ctmeta_ref, wuvt_ref,
                       o_ref, qcat_ref, m_ref, l_ref, acc_ref):
    qi = pl.program_id(1)
    tq, tk = MLA_TQ, KEY_BLOCK
    mq = MLA_HEADS * tq

    sub = lax.broadcasted_iota(jnp.int32, (ROPE_TILED, tq), 0)
    qrt = qropet_ref[...]
    for hd in range(MLA_HEADS):
        qcat_ref[0:KV_LORA, hd * tq:(hd + 1) * tq] = qlatt_ref[hd * KV_LORA:(hd + 1) * KV_LORA, :]
        own = (sub >= hd * MLA_ROPE) & (sub < (hd + 1) * MLA_ROPE)
        qcat_ref[KV_LORA:, hd * tq:(hd + 1) * tq] = jnp.where(own, qrt, jnp.zeros_like(qrt))
    qcat = qcat_ref[...]

    m_ref[...] = jnp.full(m_ref.shape, NEG_INF, F32)
    l_ref[...] = jnp.zeros(l_ref.shape, F32)
    acc_ref[...] = jnp.zeros(acc_ref.shape, F32)

    def step(kc, ct, mask):
        s = _dot(kc, qcat)
        if mask is not None:
            s = jnp.where(mask, s, NEG_INF)
        m_old = m_ref[...]
        m_new = jnp.maximum(m_old, jnp.max(s, axis=0, keepdims=True))
        alpha = jnp.exp(m_old - m_new)
        p = jnp.exp(s - m_new)
        l_ref[...] = alpha * l_ref[...] + jnp.sum(p, axis=0, keepdims=True)
        acc_ref[...] = alpha * acc_ref[...] + _dot(ct, p.astype(BF16))
        m_ref[...] = m_new

    jd = (qi * tq) // tk
    r_in = lax.broadcasted_iota(jnp.int32, (tk, mq), 0)
    c_in = lax.broadcasted_iota(jnp.int32, (tk, mq), 1) & (tq - 1)
    off = qi * tq - jd * tk
    step(kcat_ref[pl.ds(pl.multiple_of(jd * tk, tk), tk), :], ckvt_ref[jd], (r_in - c_in) <= off)

    def body(j, _):
        step(kcat_ref[pl.ds(pl.multiple_of(j * tk, tk), tk), :], ckvt_ref[j], None)
        return 0

    lax.fori_loop(0, jd, body, 0)

    tmeta = kmeta_ref.shape[0]
    mrow = lax.broadcasted_iota(jnp.int32, (tmeta, mq), 0)
    step(kmeta_ref[...], ctmeta_ref[...], mrow < N_META)

    o_lat = (acc_ref[...] / l_ref[...]).astype(BF16)
    full = _dot(wuvt_ref[...], o_lat)
    row_head = lax.broadcasted_iota(jnp.int32, (MLA_WIDTH, tq), 0) // MLA_V
    out_t = jnp.zeros((MLA_WIDTH, tq), F32)
    for hd in range(MLA_HEADS):
        out_t = out_t + jnp.where(row_head == hd, full[:, hd * tq:(hd + 1) * tq], 0.0)
    o_ref[...] = out_t.T


def _run_mla_prompt(qlatt, qropet, kcat, ckvt3, kcat_meta, ckvt_meta, wuvt, batch, seq):
    tq, tk = MLA_TQ, KEY_BLOCK
    nq = seq // tq
    nk = seq // tk
    mq = MLA_HEADS * tq
    return pl.pallas_call(
        _mla_prompt_kernel,
        grid=(batch, nq),
        in_specs=[
            pl.BlockSpec((qlatt.shape[0], tq), lambda i, j: (0, i * nq + j)),
            pl.BlockSpec((qropet.shape[0], tq), lambda i, j: (0, i * nq + j)),
            pl.BlockSpec((seq, kcat.shape[1]), lambda i, j: (i, 0)),
            pl.BlockSpec((nk, KV_LORA, tk), lambda i, j: (i, 0, 0)),
            pl.BlockSpec(kcat_meta.shape, lambda i, j: (0, 0)),
            pl.BlockSpec(ckvt_meta.shape, lambda i, j: (0, 0)),
            pl.BlockSpec(wuvt.shape, lambda i, j: (0, 0)),
        ],
        out_specs=pl.BlockSpec((tq, MLA_WIDTH), lambda i, j: (i * nq + j, 0)),
        out_shape=jax.ShapeDtypeStruct((batch * seq, MLA_WIDTH), F32),
        scratch_shapes=[
            pltpu.VMEM((KV_LORA + ROPE_TILED, mq), BF16),
            pltpu.VMEM((1, mq), F32),
            pltpu.VMEM((1, mq), F32),
            pltpu.VMEM((KV_LORA, mq), F32),
        ],
        compiler_params=pltpu.CompilerParams(
            dimension_semantics=("arbitrary", "arbitrary"), vmem_limit_bytes=VMEM_LIMIT),
        name="mla_prompt",
    )(qlatt, qropet, kcat, ckvt3, kcat_meta, ckvt_meta, wuvt)


def _rev_cumsum_blocks(lk, tri):
    rows, width = lk.shape
    cb = tri.shape[0]
    nb = width // cb
    blocks = [lk[:, b * cb:(b + 1) * cb] for b in range(nb)]
    stacked = blocks[0] if nb == 1 else jnp.concatenate(blocks, axis=0)
    hi, lo = _split_bf16(stacked)
    loc = _dot(hi, tri) + _dot(lo, tri)
    tots = jnp.sum(stacked, axis=-1, keepdims=True)
    outs = [None] * nb
    off = jnp.zeros((rows, 1), F32)
    for b in range(nb - 1, -1, -1):
        outs[b] = loc[b * rows:(b + 1) * rows, :] + off
        off = off + tots[b * rows:(b + 1) * rows, :]
    later = outs[0] if nb == 1 else jnp.concatenate(outs, axis=1)
    return later, off


def _sb_weights(z, tri, mask, carry):
    sp = _softplus(z)
    lk = -sp if mask is None else jnp.where(mask, -sp, 0.0)
    later, total = _rev_cumsum_blocks(lk, tri)
    w = jnp.exp((z - sp) + (later + carry))
    if mask is not None:
        w = jnp.where(mask, w, 0.0)
    return w.astype(BF16), carry + total


def _softmax_update(s, mask, m, l):
    if mask is not None:
        s = jnp.where(mask, s, NEG_INF)
    m_new = jnp.maximum(m, jnp.max(s, axis=-1, keepdims=True))
    alpha = jnp.exp(m - m_new)
    p = jnp.exp(s - m_new)
    return p.astype(BF16), alpha, m_new, alpha * l + jnp.sum(p, axis=-1, keepdims=True)


def _head_diag_rows(full):
    rows, width = full.shape
    nq = rows // SB_HEADS
    row_head = lax.broadcasted_iota(jnp.int32, (rows, width), 0) // nq
    col_head = lax.broadcasted_iota(jnp.int32, (rows, width), 1) // SB_HEAD_DIM
    kept = jnp.where(row_head == col_head, full, 0.0)
    out = kept[0:nq, :]
    for hd in range(1, SB_HEADS):
        out = out + kept[hd * nq:(hd + 1) * nq, :]
    return out


def _decode_kernel(pt_ref, qbd_ref, qlat_ref, qrope_ref, knew_ref, vnew_ref, cnew_ref, rnew_ref,
                   tri_ref, wuv_ref, ck_hbm, cv_hbm, cc_hbm, cr_hbm, oa_ref, ob_ref,
                   kbuf, vbuf, cbuf, rbuf, acct_ref, sem, sem_sb):
    s = pl.program_id(0)
    n_seq = pl.num_programs(0)
    n_pages = pt_ref.shape[1]
    g = PAGES_PER_CHUNK
    n_chunks = n_pages // g
    page = cbuf.shape[1] // g
    gs = SB_PAGES_PER_CHUNK
    n_sb_chunks = n_pages // gs

    def copies(seq, chunk, slot):
        out = []
        for p in range(g):
            pg = pt_ref[seq, chunk * g + p]
            span = pl.ds(p * page, page)
            out.append(pltpu.make_async_copy(cc_hbm.at[pg], cbuf.at[slot, span, :], sem.at[slot]))
            out.append(pltpu.make_async_copy(cr_hbm.at[pg], rbuf.at[slot, :, span], sem.at[slot]))
        return out

    def sb_copies(seq, k, slot):
        out = []
        for p in range(gs):
            pg = pt_ref[seq, n_pages - (k + 1) * gs + p]
            span = pl.ds(p * page, page)
            out.append(pltpu.make_async_copy(ck_hbm.at[pg], kbuf.at[slot, :, span], sem_sb.at[slot]))
            out.append(pltpu.make_async_copy(cv_hbm.at[pg], vbuf.at[slot, :, span], sem_sb.at[slot]))
        return out

    def start_all(cs):
        for c in cs:
            c.start()

    def wait_all(cs):
        for c in cs:
            c.wait()

    def start_global(gidx):
        seq = lax.div(gidx, n_chunks)
        start_all(copies(seq, n_chunks - 1 - lax.rem(gidx, n_chunks), lax.rem(gidx, N_SLOTS)))

    @pl.when(s == 0)
    def _():
        start_all(sb_copies(0, 0, 0))
        for ahead in range(N_SLOTS - 1):
            start_all(copies(ahead // n_chunks, n_chunks - 1 - ahead % n_chunks, ahead % N_SLOTS))

    @pl.when(s + 1 < n_seq)
    def _():
        start_all(sb_copies(s + 1, 0, (s + 1) & 1))

    qbd = qbd_ref[...]
    qlat = qlat_ref[...]
    qrope = qrope_ref[...]
    tri = tri_ref[...]
    rows = qbd.shape[0]
    nq = rows // SB_HEADS

    n_new = knew_ref.shape[0]

    def pad_new(ref):
        x = ref[...]
        return jnp.concatenate([x, jnp.zeros((page - n_new, x.shape[1]), x.dtype)], axis=0).astype(BF16)

    q_idx = lax.broadcasted_iota(jnp.int32, (rows, page), 0) & (nq - 1)
    k_idx = lax.broadcasted_iota(jnp.int32, (rows, page), 1)
    w, carry = _sb_weights(_dot_nt(qbd, pad_new(knew_ref)), tri_ref[0:page, 0:page], k_idx < q_idx,
                           jnp.zeros((rows, 1), F32))
    oa_new = _dot(w, pad_new(vnew_ref))
    cnew = pad_new(cnew_ref)
    p, _, m, l = _softmax_update(_dot_nt(qlat, cnew) + _dot_nt(qrope, pad_new(rnew_ref)),
                                 k_idx <= q_idx, jnp.full((rows, 1), NEG_INF, F32),
                                 jnp.zeros((rows, 1), F32))
    acc_b = _dot(p, cnew)
    acct_ref[...] = jnp.zeros(acct_ref.shape, F32)

    def sb_chunk(slot, carry):
        w, carry = _sb_weights(_dot(qbd, kbuf[slot].astype(BF16)), tri, None, carry)
        acct_ref[:, 0:rows] += _dot_nt(vbuf[slot].astype(BF16), w)
        return carry

    first = s & 1
    wait_all(sb_copies(s, 0, first))
    carry = sb_chunk(first, carry)

    def sb_more(state):
        k, carry = state
        return jnp.logical_and(k < n_sb_chunks, jnp.max(carry) > -SB_EXIT)

    def sb_body(state):
        k, carry = state
        start_all(sb_copies(s, k, 2))
        wait_all(sb_copies(s, k, 2))
        return k + 1, sb_chunk(2, carry)

    lax.while_loop(sb_more, sb_body, (jnp.int32(1), carry))

    def body(k, state):
        m, l, acc_b = state
        chunk = n_chunks - 1 - k
        gidx = s * n_chunks + k
        slot = lax.rem(gidx, N_SLOTS)

        @pl.when(gidx + (N_SLOTS - 1) < n_seq * n_chunks)
        def _():
            start_global(gidx + (N_SLOTS - 1))

        wait_all(copies(s, chunk, slot))
        cc = cbuf[slot].astype(BF16)
        p, alpha, m, l = _softmax_update(_dot_nt(qlat, cc) + _dot(qrope, rbuf[slot].astype(BF16)),
                                         None, m, l)
        acc_b = alpha * acc_b + _dot(p, cc)
        return m, l, acc_b

    m, l, acc_b = lax.fori_loop(0, n_chunks, body, (m, l, acc_b))

    oa_ref[...] = _head_diag_rows(acct_ref[...].T[0:rows, :] + oa_new)
    o_lat = (acc_b / l).astype(BF16)
    ob_ref[...] = _head_diag_rows(_dot(o_lat, wuv_ref[...]))


def _run_decode(page_table, qbd, qlat, qrope, knew, vnew, cnew, rnew, tri, wuv_all,
                cache_kt, cache_vt, cache_c, cache_rt):
    n_seq, n_pages = page_table.shape
    page = cache_c.shape[1]
    rows = qbd.shape[1]
    nq = rows // SB_HEADS
    n_new = knew.shape[1]
    tok = PAGES_PER_CHUNK * page
    sb_tok = SB_PAGES_PER_CHUNK * page
    per_seq = lambda i, pt: (i, 0, 0)
    const2 = lambda i, pt: (0, 0)
    grid_spec = pltpu.PrefetchScalarGridSpec(
        num_scalar_prefetch=1,
        grid=(n_seq,),
        in_specs=[
            pl.BlockSpec((None, rows, SB_WIDTH), per_seq),
            pl.BlockSpec((None, rows, KV_LORA), per_seq),
            pl.BlockSpec((None, rows, MLA_ROPE), per_seq),
            pl.BlockSpec((None, n_new, SB_WIDTH), per_seq),
            pl.BlockSpec((None, n_new, SB_WIDTH), per_seq),
            pl.BlockSpec((None, n_new, KV_LORA), per_seq),
            pl.BlockSpec((None, n_new, MLA_ROPE), per_seq),
            pl.BlockSpec(tri.shape, const2),
            pl.BlockSpec(wuv_all.shape, const2),
            pl.BlockSpec(memory_space=pl.ANY),
            pl.BlockSpec(memory_space=pl.ANY),
            pl.BlockSpec(memory_space=pl.ANY),
            pl.BlockSpec(memory_space=pl.ANY),
        ],
        out_specs=[
            pl.BlockSpec((None, nq, SB_WIDTH), per_seq),
            pl.BlockSpec((None, nq, MLA_WIDTH), per_seq),
        ],
        scratch_shapes=[
            pltpu.VMEM((3, SB_WIDTH, sb_tok), F32),
            pltpu.VMEM((3, SB_WIDTH, sb_tok), F32),
            pltpu.VMEM((N_SLOTS, tok, KV_LORA), F32),
            pltpu.VMEM((N_SLOTS, MLA_ROPE, tok), F32),
            pltpu.VMEM((SB_WIDTH, LANES), F32),
            pltpu.SemaphoreType.DMA((N_SLOTS,)),
            pltpu.SemaphoreType.DMA((3,)),
        ],
    )
    return pl.pallas_call(
        _decode_kernel,
        grid_spec=grid_spec,
        out_shape=[jax.ShapeDtypeStruct((n_seq, nq, SB_WIDTH), F32),
                   jax.ShapeDtypeStruct((n_seq, nq, MLA_WIDTH), F32)],
        compiler_params=pltpu.CompilerParams(
            dimension_semantics=("arbitrary",), vmem_limit_bytes=VMEM_LIMIT),
        name="decode",
    )(page_table, qbd, qlat, qrope, knew, vnew, cnew, rnew, tri, wuv_all,
      cache_kt, cache_vt, cache_c, cache_rt)


def _out_kernel(x_ref, oa_ref, ob_ref, sza_ref, szb_ref, ga_ref, gb_ref, wpa_ref, wpb_ref,
                wout_ref, gpost_ref, y_ref):
    a = (oa_ref[...] * sza_ref[...]).astype(BF16)
    b = (ob_ref[...] * szb_ref[...]).astype(BF16)
    merged = ga_ref[...] * _dot(a, wpa_ref[...]) + gb_ref[...] * _dot(b, wpb_ref[...])
    t = _dot(merged.astype(BF16), wout_ref[...])
    y_ref[...] = x_ref[...] + _rms(t, gpost_ref[...])


def _run_out(x2d, oa, ob, sza, szb, ga, gb, wpa, wpb, wout, gpost, tm):
    rows = x2d.shape[0]
    row = lambda t: (t, 0)
    const2 = lambda t: (0, 0)
    return pl.pallas_call(
        _out_kernel,
        grid=(rows // tm,),
        in_specs=[
            pl.BlockSpec((tm, D_MODEL), row),
            pl.BlockSpec((tm, SB_WIDTH), row),
            pl.BlockSpec((tm, MLA_WIDTH), row),
            pl.BlockSpec((tm, SB_WIDTH), row),
            pl.BlockSpec((tm, MLA_WIDTH), row),
            pl.BlockSpec((tm, D_MODEL), row),
            pl.BlockSpec((tm, D_MODEL), row),
            pl.BlockSpec(wpa.shape, const2),
            pl.BlockSpec(wpb.shape, const2),
            pl.BlockSpec(wout.shape, const2),
            pl.BlockSpec((1, D_MODEL), const2),
        ],
        out_specs=pl.BlockSpec((tm, D_MODEL), row),
        out_shape=jax.ShapeDtypeStruct((rows, D_MODEL), F32),
        compiler_params=pltpu.CompilerParams(
            dimension_semantics=("arbitrary",), vmem_limit_bytes=VMEM_LIMIT),
        name="out_mix",
    )(x2d, oa, ob, sza, szb, ga, gb, wpa, wpb, wout, gpost)


def _rope_tables(pos):
    half = MLA_ROPE // 2
    inv = ROPE_THETA ** (-jnp.arange(half, dtype=jnp.float32) * 2.0 / MLA_ROPE)
    ang = pos.astype(jnp.float32)[:, None] * inv[None, :]
    cos, sin = jnp.cos(ang), jnp.sin(ang)
    cos_t = jnp.repeat(cos, 2, axis=1)
    sin_t = jnp.stack([-sin, sin], axis=-1).reshape(pos.shape[0], MLA_ROPE)
    return jnp.tile(cos_t, (1, MLA_HEADS)), jnp.tile(sin_t, (1, MLA_HEADS))


def kernel(x_prompt, x_sample, cache_sb_k, cache_sb_v, cache_mla_ckv, cache_mla_krope, page_table,
           meta_tokens, g_pre, w_in, g_qnorm, w_uq, g_kvnorm, w_uk, w_uv, w_proj_a, w_proj_b,
           w_out, g_post):
    assert g_pre.shape[0] == 1, "single layer step"
    batch, seq, d_model = x_prompt.shape
    n_dec, dec_seq, _ = x_sample.shape
    n_pool, page = cache_sb_k.shape[1], cache_sb_k.shape[2]
    past_len = page_table.shape[1] * page
    pair_swap = jnp.arange(MLA_ROPE) ^ 1

    w = w_in[0]
    o_ckv = 4 * SB_WIDTH + Q_LORA
    o_kr = o_ckv + KV_LORA
    w_kr = w[:, o_kr:o_kr + MLA_ROPE]
    w_aug = jnp.concatenate(
        [w[:, :o_kr], w[:, o_kr + MLA_ROPE:], jnp.tile(w_kr, (1, MLA_HEADS)),
         jnp.tile(w_kr[:, pair_swap], (1, MLA_HEADS))], axis=1).astype(BF16)
    assert w_aug.shape[1] == _N_AUG
    wqt = w[:, :SB_WIDTH].T.astype(BF16)
    wvt = w[:, 2 * SB_WIDTH:3 * SB_WIDTH].T.astype(BF16)
    wct = w[:, o_ckv:o_ckv + KV_LORA].T.astype(BF16)
    wq = w_uq[0].reshape(Q_LORA, MLA_HEADS, MLA_NOPE + MLA_ROPE)
    wq_nope = jnp.pad(wq[:, :, :MLA_NOPE], ((0, 0), (0, 0), (0, NOPE_PAD - MLA_NOPE)))
    wq_rope = wq[:, :, MLA_NOPE:]
    wuqt = jnp.concatenate(
        [wq_nope.reshape(Q_LORA, -1), wq_rope.reshape(Q_LORA, -1),
         wq_rope[:, :, pair_swap].reshape(Q_LORA, -1)], axis=1).T.astype(BF16)
    assert wuqt.shape[0] == _N_UQ
    wuk_pad = jnp.pad(w_uk[0], ((0, 0), (0, 0), (0, NOPE_PAD - MLA_NOPE))).astype(BF16)
    wuv_all = jnp.swapaxes(w_uv[0], 0, 1).reshape(KV_LORA, MLA_WIDTH).astype(BF16)
    wuvt = wuv_all.T
    wpa = w_proj_a[0].astype(BF16)
    wpb = w_proj_b[0].astype(BF16)
    wout = w_out[0].astype(BF16)
    idx = jnp.arange(KEY_BLOCK)
    tri = (idx[:, None] > idx[None, :]).astype(BF16)
    weights = (g_pre, w_aug, wqt, wvt, wct, g_qnorm, wuqt, wuk_pad, g_kvnorm, g_kvnorm.reshape(KV_LORA, 1))

    n_s = n_dec * dec_seq
    n_small = -(-(n_s + N_META) // SMALL_TM) * SMALL_TM
    xs2 = x_sample.reshape(n_s, d_model)
    x_small = jnp.concatenate([xs2, meta_tokens.astype(x_prompt.dtype),
                               jnp.zeros((n_small - n_s - N_META, d_model), x_prompt.dtype)], axis=0)
    pos_small = jnp.concatenate([past_len + (jnp.arange(n_s, dtype=jnp.int32) % dec_seq),
                                 jnp.arange(n_small - n_s, dtype=jnp.int32)])
    cos_s, sin_s = _rope_tables(pos_small)
    proj_s = _run_proj(x_small, cos_s, sin_s, weights, SMALL_TM)
    (qt_s, k_s, kbf_s, v_s, vt_s, sza_s, szb_s, ga_s, gb_s, qlatt_s, qropet_s, ckv_s, kcat_s,
     ckvt_s, krope_s) = proj_s

    def meta_first(small):
        meta = jnp.broadcast_to(small[n_s:n_s + N_META][None], (batch, N_META, small.shape[-1]))
        return jnp.pad(meta, ((0, 0), (0, seq), (0, 0))).reshape(batch * (N_META + seq), small.shape[-1])

    xp2 = x_prompt.reshape(batch * seq, d_model)
    cos_p, sin_p = _rope_tables(N_META + jnp.arange(seq, dtype=jnp.int32))
    proj_p = _run_proj(xp2, cos_p, sin_p, weights, PROJ_TM, batch=batch, lead=N_META,
                       bases=[meta_first(a) for a in (k_s, v_s, ckv_s, krope_s)])
    (qt_p, k_p, kbf_p, v_p, vt_p, sza_p, szb_p, ga_p, gb_p, qlatt_p, qropet_p, ckv_p, kcat_p,
     ckvt_p, krope_p) = proj_p
    assert n_s % SMALL_TM == 0
    meta_blk = n_s // SMALL_TM

    oa_p = _run_sb_prompt(qt_p, kbf_p, vt_p, kbf_s[n_s:n_s + SMALL_TM], vt_s[meta_blk], tri.T,
                          batch, seq)
    ob_p = _run_mla_prompt(qlatt_p, qropet_p, kcat_p, ckvt_p, kcat_s[n_s:n_s + SMALL_TM],
                           ckvt_s[meta_blk], wuvt, batch, seq)
    y_p = _run_out(xp2, oa_p, ob_p, sza_p, szb_p, ga_p, gb_p, wpa, wpb, wout, g_post, OUT_TM)

    rows = SB_HEADS * dec_seq
    q4 = qt_s[:, :n_s].T.reshape(n_dec, dec_seq, SB_WIDTH)
    qbd = jnp.tile(q4, (1, SB_HEADS, 1)).reshape(n_dec, SB_HEADS, dec_seq, SB_WIDTH)
    own = (jnp.arange(SB_WIDTH)[None, :] // SB_HEAD_DIM) == jnp.arange(SB_HEADS)[:, None]
    qbd = jnp.where(own[None, :, None, :], qbd, jnp.zeros_like(qbd)).reshape(n_dec, rows, SB_WIDTH)
    qlat_d = qlatt_s[:, :n_s].reshape(MLA_HEADS, KV_LORA, n_dec, dec_seq).transpose(2, 0, 3, 1)
    qlat_d = qlat_d.reshape(n_dec, rows, KV_LORA)
    qrope_d = qropet_s[:, :n_s].reshape(MLA_HEADS, MLA_ROPE, n_dec, dec_seq).transpose(2, 0, 3, 1)
    qrope_d = qrope_d.reshape(n_dec, rows, MLA_ROPE)

    def new_rows(a):
        a = a[:n_s].reshape(n_dec, dec_seq, a.shape[-1])
        return jnp.pad(a, ((0, 0), (0, 8 - dec_seq), (0, 0)))

    cache_kt = cache_sb_k[0].transpose(0, 2, 3, 1).reshape(n_pool, SB_WIDTH, page)
    cache_vt = cache_sb_v[0].transpose(0, 2, 3, 1).reshape(n_pool, SB_WIDTH, page)
    cache_rt = cache_mla_krope[0].transpose(0, 2, 1)
    oa_s, ob_s = _run_decode(
        page_table, qbd, qlat_d, qrope_d, new_rows(k_s), new_rows(v_s), new_rows(ckv_s),
        new_rows(krope_s), tri, wuv_all, cache_kt, cache_vt, cache_mla_ckv[0], cache_rt)
    y_s = _run_out(xs2, oa_s.reshape(n_s, SB_WIDTH), ob_s.reshape(n_s, MLA_WIDTH), sza_s[:n_s],
                   szb_s[:n_s], ga_s[:n_s], gb_s[:n_s], wpa, wpb, wout, g_post, n_s)

    lp = seq + N_META
    return (
        y_p.reshape(batch, seq, d_model),
        y_s.reshape(n_dec, dec_seq, d_model),
        k_p.reshape(1, batch, lp, SB_HEADS, SB_HEAD_DIM),
        v_p.reshape(1, batch, lp, SB_HEADS, SB_HEAD_DIM),
        ckv_p.reshape(1, batch, lp, KV_LORA),
        krope_p.reshape(1, batch, lp, MLA_ROPE),
        k_s[:n_s].reshape(1, n_dec, dec_seq, SB_HEADS, SB_HEAD_DIM),
        v_s[:n_s].reshape(1, n_dec, dec_seq, SB_HEADS, SB_HEAD_DIM),
        ckv_s[:n_s].reshape(1, n_dec, dec_seq, KV_LORA),
        krope_s[:n_s].reshape(1, n_dec, dec_seq, MLA_ROPE),
    )
```
